```python
import math
import jax, jax.numpy as jnp
from jax import lax
import numpy as np

D_MODEL = 1024
BATCH = 4
SEQ = 8192
DEPTH = 1

PLE_DIM = 256
D_POOL = D_MODEL // 2
POOL_WINDOWS = (2, 4, 8, 16)
N_POOL_GROUPS = len(POOL_WINDOWS)
POOL_GROUP_DIM = D_POOL // N_POOL_GROUPS
SB_HEADS = 8
SB_HEAD_DIM = 64
D_SB = SB_HEADS * SB_HEAD_DIM
Q_BLOCK = 128
D_FF = ((8 * D_MODEL // 3 + 255) // 256) * 256
D_IN = D_POOL + 3 * D_SB + 2 * D_MODEL
RMS_EPS = 1e-6

kernel_name = "hybrid_pool_stickbreaking_gated_block"


def rms_norm(x, gain):
    xf = x.astype(jnp.float32)
    inv = lax.rsqrt(jnp.mean(xf * xf, axis=-1, keepdims=True) + RMS_EPS)
    return (xf * inv * gain.astype(jnp.float32)).astype(x.dtype)


def pool_mixer(u, w_pool, pool_scale):
    B, S, _ = u.shape
    uf = u.astype(jnp.float32)
    csum = jnp.cumsum(uf, axis=1)
    t = jnp.arange(S, dtype=jnp.int32)
    outs = []
    for g, w in enumerate(POOL_WINDOWS):
        sl = slice(g * POOL_GROUP_DIM, (g + 1) * POOL_GROUP_DIM)
        cg = csum[..., sl]
        prev = jnp.pad(cg, ((0, 0), (w, 0), (0, 0)))[:, :S]
        count = jnp.minimum(t + 1, w).astype(jnp.float32)[None, :, None]
        outs.append((cg - prev) / count - uf[..., sl])
    pooled = jnp.stack(outs, axis=2).astype(u.dtype)
    mixed = jnp.einsum('bsgc,gcd->bsgd', pooled, w_pool).reshape(B, S, D_POOL)
    return mixed * pool_scale


def stick_breaking_attention(q, k, v):
    B, H, S, dh = q.shape
    n_blk = S // Q_BLOCK
    scale = 1.0 / math.sqrt(dh)
    q_blocks = q.reshape(B, H, n_blk, Q_BLOCK, dh).transpose(2, 0, 1, 3, 4)
    starts = jnp.arange(n_blk, dtype=jnp.int32) * Q_BLOCK
    k_pos = jnp.arange(S, dtype=jnp.int32)

    def one_block(args):
        q_i, start = args
        z = jnp.einsum('bhqd,bhkd->bhqk', q_i, k).astype(jnp.float32) * scale
        t_pos = start + jnp.arange(Q_BLOCK, dtype=jnp.int32)
        mask = k_pos[None, :] < t_pos[:, None]
        log_fail = jnp.where(mask, jax.nn.log_sigmoid(-z), 0.0)
        suffix = lax.cumsum(log_fail, axis=3, reverse=True) - log_fail
        a = jnp.where(mask, jnp.exp(jax.nn.log_sigmoid(z) + suffix), 0.0)
        return jnp.einsum('bhqk,bhkd->bhqd', a.astype(v.dtype), v)

    out = lax.map(one_block, (q_blocks, starts))
    return out.transpose(1, 2, 0, 3, 4).reshape(B, H, S, dh)


def setup_inputs(seed: int = 0) -> dict:
    key = jax.random.key(seed)
    ks = jax.random.split(key, 20)
    f32 = jnp.float32

    def nrm(k, shape, fan_in):
        return jax.random.normal(k, shape, f32) * (fan_in ** -0.5)

    def gain(k, shape):
        return jnp.ones(shape, f32) + 0.02 * jax.random.normal(k, shape, f32)

    return {
        "x": jax.random.normal(ks[0], (BATCH, SEQ, D_MODEL), f32),
        "p": jax.random.normal(ks[1], (DEPTH, BATCH, SEQ, PLE_DIM), f32),
        "norm_mix": gain(ks[2], (DEPTH, D_MODEL)),
        "w_in": nrm(ks[3], (DEPTH, D_MODEL, D_IN), D_MODEL),
        "w_pool": nrm(ks[4], (DEPTH, N_POOL_GROUPS, POOL_GROUP_DIM, POOL_GROUP_DIM), POOL_GROUP_DIM),
        "pool_scale": gain(ks[5], (DEPTH, D_POOL)),
        "w_branch_a": nrm(ks[6], (DEPTH, D_POOL, D_MODEL), D_POOL),
        "w_branch_b": nrm(ks[7], (DEPTH, D_SB, D_MODEL), D_SB),
        "w_out": nrm(ks[8], (DEPTH, D_MODEL, D_MODEL), D_MODEL),
        "norm_ffn": gain(ks[9], (DEPTH, D_MODEL)),
        "w_ffn_gate": nrm(ks[10], (DEPTH, D_MODEL, D_FF), D_MODEL),
        "w_ffn_up": nrm(ks[11], (DEPTH, D_MODEL, D_FF), D_MODEL),
        "w_ffn_down": nrm(ks[12], (DEPTH, D_FF, D_MODEL), D_FF),
        "norm_ple": gain(ks[13], (DEPTH, D_MODEL)),
        "w_ple_gate": nrm(ks[14], (DEPTH, D_MODEL, D_MODEL), D_MODEL),
        "w_ple_proj": nrm(ks[15], (DEPTH, PLE_DIM, D_MODEL), PLE_DIM),
        "norm_final": gain(ks[16], (D_MODEL,)),
    }


def reference(x, p, norm_mix, w_in, w_pool, pool_scale, w_branch_a, w_branch_b, w_out,
              norm_ffn, w_ffn_gate, w_ffn_up, w_ffn_down, norm_ple, w_ple_gate, w_ple_proj,
              norm_final):
    B, S, _ = x.shape
    split_at = [D_POOL, D_POOL + D_SB, D_POOL + 2 * D_SB, D_POOL + 3 * D_SB,
                D_POOL + 3 * D_SB + D_MODEL]
    for i in range(DEPTH):
        h = rms_norm(x, norm_mix[i])
        proj = h @ w_in[i]
        u_pool, q, k, v, g_a, g_b = jnp.split(proj, split_at, axis=-1)
        y_a = pool_mixer(u_pool, w_pool[i], pool_scale[i])
        to_heads = lambda t: t.reshape(B, S, SB_HEADS, SB_HEAD_DIM).transpose(0, 2, 1, 3)
        y_b = stick_breaking_attention(to_heads(q), to_heads(k), to_heads(v))
        y_b = y_b.transpose(0, 2, 1, 3).reshape(B, S, D_SB)
        merged = (jax.nn.sigmoid(g_a) * (y_a @ w_branch_a[i])
                  + jax.nn.sigmoid(g_b) * (y_b @ w_branch_b[i]))
        x = x + merged @ w_out[i]
        h = rms_norm(x, norm_ffn[i])
        x = x + (jax.nn.silu(h @ w_ffn_gate[i]) * (h @ w_ffn_up[i])) @ w_ffn_down[i]
        gate = jax.nn.sigmoid(rms_norm(x, norm_ple[i]) @ w_ple_gate[i])
        x = x + gate * (p[i] @ w_ple_proj[i])
    return rms_norm(x, norm_final)
```

```python
import functools

import jax
import jax.numpy as jnp
from jax import lax
from jax.experimental import pallas as pl
from jax.experimental.pallas import tpu as pltpu

D_MODEL = 1024
PLE_DIM = 256
D_POOL = D_MODEL // 2
POOL_WINDOWS = (2, 4, 8, 16)
POOL_GROUP_DIM = D_POOL // len(POOL_WINDOWS)
POOL_HALO = 16
SB_HEADS = 8
SB_HEAD_DIM = 64
D_SB = SB_HEADS * SB_HEAD_DIM
D_FF = ((8 * D_MODEL // 3 + 255) // 256) * 256
D_IN = D_POOL + 3 * D_SB + 2 * D_MODEL
RMS_EPS = 1e-6

LANES = 128
ROW_TILE = 512
Q_BLOCK = 256
K_BLOCK = 256
VMEM_LIMIT = 56 * 1024 * 1024

F32 = jnp.float32
BF16 = jnp.bfloat16


def _dot(a, b):
    return jnp.dot(a, b, preferred_element_type=F32)


def _rms(xf, gain_row):
    inv = lax.rsqrt(jnp.mean(xf * xf, axis=-1, keepdims=True) + RMS_EPS)
    return xf * inv * gain_row


def _resident(shape):
    zeros = (0,) * len(shape)
    return pl.BlockSpec(shape, lambda *_: zeros, pipeline_mode=pl.Buffered(1))


def _proj_kernel(x_ref, gain_ref, w_ref, u_ref, qkv_ref, g_ref):
    h = _rms(x_ref[...], gain_ref[...]).astype(BF16)
    u_ref[...] = _dot(h, w_ref[:, 0:D_POOL])
    q = _dot(h, w_ref[:, D_POOL:D_POOL + D_SB]) * (-1.0 / 8.0)
    qkv_ref[:, 0:D_SB] = q.astype(BF16)
    qkv_ref[:, D_SB:3 * D_SB] = _dot(h, w_ref[:, D_POOL + D_SB:D_POOL + 3 * D_SB]).astype(BF16)
    g_ref[...] = _dot(h, w_ref[:, D_POOL + 3 * D_SB:D_IN])


def _proj(x2d, gain, w_in):
    t = x2d.shape[0]
    return pl.pallas_call(
        _proj_kernel,
        grid=(t // ROW_TILE,),
        in_specs=[
            pl.BlockSpec((ROW_TILE, D_MODEL), lambda i: (i, 0)),
            _resident((1, D_MODEL)),
            _resident((D_MODEL, D_IN)),
        ],
        out_specs=[
            pl.BlockSpec((ROW_TILE, D_POOL), lambda i: (i, 0)),
            pl.BlockSpec((ROW_TILE, 3 * D_SB), lambda i: (i, 0)),
            pl.BlockSpec((ROW_TILE, 2 * D_MODEL), lambda i: (i, 0)),
        ],
        out_shape=[
            jax.ShapeDtypeStruct((t, D_POOL), F32),
            jax.ShapeDtypeStruct((t, 3 * D_SB), BF16),
            jax.ShapeDtypeStruct((t, 2 * D_MODEL), F32),
        ],
        compiler_params=pltpu.CompilerParams(
            dimension_semantics=("arbitrary",), vmem_limit_bytes=VMEM_LIMIT),
        name="proj",
    )(x2d, gain, w_in)


def _attn_kernel(q_ref, k_ref, v_ref, tri_ref, o_ref, acc_ref, carry_ref):
    qi = pl.program_id(2)
    rows = 2 * Q_BLOCK

    q2 = q_ref[...]
    lane = lax.broadcasted_iota(jnp.int32, (Q_BLOCK, LANES), 1)
    zero = jnp.zeros_like(q2)
    qs = jnp.concatenate([jnp.where(lane < SB_HEAD_DIM, q2, zero),
                          jnp.where(lane >= SB_HEAD_DIM, q2, zero)], axis=0)
    tri = tri_ref[...]

    acc_ref[...] = jnp.zeros_like(acc_ref)
    carry_ref[...] = jnp.zeros_like(carry_ref)

    def sweep(kb, diagonal):
        start = pl.multiple_of(kb * K_BLOCK, K_BLOCK)
        kblk = k_ref[pl.ds(start, K_BLOCK), :]
        vblk = v_ref[pl.ds(start, K_BLOCK), :]
        nz = lax.dot_general(qs, kblk, (((1,), (1,)), ((), ())),
                             preferred_element_type=F32)
        lf = jnp.minimum(nz, 0.0) - jnp.log(1.0 + jnp.exp(-jnp.abs(nz)))
        if diagonal:
            t_loc = lax.broadcasted_iota(jnp.int32, (rows, K_BLOCK), 0) & (Q_BLOCK - 1)
            s_loc = lax.broadcasted_iota(jnp.int32, (rows, K_BLOCK), 1)
            mask = s_loc < t_loc
            lf = jnp.where(mask, lf, 0.0)
        hi = lf.astype(BF16)
        lo = (lf - hi.astype(F32)).astype(BF16)
        suffix = _dot(hi, tri) + _dot(lo, tri)
        carry = carry_ref[...]
        a = jnp.exp((lf - nz) + suffix + carry)
        if diagonal:
            a = jnp.where(mask, a, 0.0)
        acc_ref[...] += _dot(a.astype(BF16), vblk)
        carry_ref[...] = carry + jnp.sum(lf, axis=1, keepdims=True)

    sweep(qi, True)

    def body(i, c):
        sweep(qi - 1 - i, False)
        return c

    lax.fori_loop(0, qi, body, 0)

    acc = acc_ref[...]
    o_ref[...] = jnp.where(lane < SB_HEAD_DIM, acc[:Q_BLOCK], acc[Q_BLOCK:]).astype(o_ref.dtype)


def _attention(qkv, batch, seq):
    t = batch * seq
    n_q = seq // Q_BLOCK
    n_pair = D_SB // LANES
    jj = lax.broadcasted_iota(jnp.int32, (K_BLOCK, K_BLOCK), 0)
    ss = lax.broadcasted_iota(jnp.int32, (K_BLOCK, K_BLOCK), 1)
    tri = (jj > ss).astype(BF16)
    return pl.pallas_call(
        _attn_kernel,
        grid=(batch, n_pair, n_q),
        in_specs=[
            pl.BlockSpec((Q_BLOCK, LANES), lambda b, j, i: (b * n_q + i, j)),
            pl.BlockSpec((seq, LANES), lambda b, j, i: (b, n_pair + j)),
            pl.BlockSpec((seq, LANES), lambda b, j, i: (b, 2 * n_pair + j)),
            _resident((K_BLOCK, K_BLOCK)),
        ],
        out_specs=pl.BlockSpec((Q_BLOCK, LANES), lambda b, j, i: (b * n_q + i, j)),
        out_shape=jax.ShapeDtypeStruct((t, D_SB), BF16),
        scratch_shapes=[
            pltpu.VMEM((2 * Q_BLOCK, LANES), F32),
            pltpu.VMEM((2 * Q_BLOCK, 1), F32),
        ],
        compiler_params=pltpu.CompilerParams(
            dimension_semantics=("arbitrary", "arbitrary", "arbitrary"),
            vmem_limit_bytes=VMEM_LIMIT),
        name="attn",
    )(qkv, qkv, qkv, tri)


def _merge_kernel(seq, u_ref, halo_ref, yb_ref, g_ref, x_ref, wpool_ref, pscale_ref,
                  wa_ref, wb_ref, wout_ref, o_ref, ubuf_ref):
    i = pl.program_id(0)
    t0 = (i * ROW_TILE) % seq
    u = u_ref[...]
    halo = halo_ref[...]
    ubuf_ref[0:POOL_HALO, :] = jnp.where(t0 == 0, jnp.zeros_like(halo), halo)
    ubuf_ref[POOL_HALO:, :] = u
    pos = t0 + lax.broadcasted_iota(jnp.int32, (ROW_TILE, 1), 0)

    mixed = []
    for g, w in enumerate(POOL_WINDOWS):
        cols = slice(g * POOL_GROUP_DIM, (g + 1) * POOL_GROUP_DIM)
        win = u[:, cols]
        for j in range(1, w):
            win = win + ubuf_ref[POOL_HALO - j:POOL_HALO - j + ROW_TILE, cols]
        count = jnp.minimum(pos + 1, w).astype(F32)
        pooled = win / count - u[:, cols]
        mixed.append(_dot(pooled.astype(BF16), wpool_ref[g]))
    y_a = jnp.concatenate(mixed, axis=1) * pscale_ref[...]

    g_all = g_ref[...]
    merged = (jax.nn.sigmoid(g_all[:, :D_MODEL]) * _dot(y_a.astype(BF16), wa_ref[...])
              + jax.nn.sigmoid(g_all[:, D_MODEL:]) * _dot(yb_ref[...], wb_ref[...]))
    o_ref[...] = x_ref[...] + _dot(merged.astype(BF16), wout_ref[...])


def _merge(u, y_b, g, x2d, w_pool, pool_scale, w_a, w_b, w_out, seq):
    t = x2d.shape[0]
    halo_per_tile = ROW_TILE // POOL_HALO
    return pl.pallas_call(
        functools.partial(_merge_kernel, seq),
        grid=(t // ROW_TILE,),
        in_specs=[
            pl.BlockSpec((ROW_TILE, D_POOL), lambda i: (i, 0)),
            pl.BlockSpec((POOL_HALO, D_POOL), lambda i: (jnp.maximum(i * halo_per_tile - 1, 0), 0)),
            pl.BlockSpec((ROW_TILE, D_SB), lambda i: (i, 0)),
            pl.BlockSpec((ROW_TILE, 2 * D_MODEL), lambda i: (i, 0)),
            pl.BlockSpec((ROW_TILE, D_MODEL), lambda i: (i, 0)),
            _resident(w_pool.shape),
            _resident((1, D_POOL)),
            _resident((D_POOL, D_MODEL)),
            _resident((D_SB, D_MODEL)),
            _resident((D_MODEL, D_MODEL)),
        ],
        out_specs=pl.BlockSpec((ROW_TILE, D_MODEL), lambda i: (i, 0)),
        out_shape=jax.ShapeDtypeStruct((t, D_MODEL), F32),
        scratch_shapes=[pltpu.VMEM((ROW_TILE + POOL_HALO, D_POOL), F32)],
        compiler_params=pltpu.CompilerParams(
            dimension_semantics=("arbitrary",), vmem_limit_bytes=VMEM_LIMIT),
        name="merge",
    )(u, u, y_b, g, x2d, w_pool, pool_scale, w_a, w_b, w_out)


def _ffn_kernel(x_ref, gain_ref, wg_ref, wu_ref, wd_ref, o_ref):
    x = x_ref[...]
    h = _rms(x, gain_ref[...]).astype(BF16)
    act = jax.nn.silu(_dot(h, wg_ref[...])) * _dot(h, wu_ref[...])
    o_ref[...] = x + _dot(act.astype(BF16), wd_ref[...])


def _ffn(x2d, gain, w_gate, w_up, w_down):
    t = x2d.shape[0]
    return pl.pallas_call(
        _ffn_kernel,
        grid=(t // ROW_TILE,),
        in_specs=[
            pl.BlockSpec((ROW_TILE, D_MODEL), lambda i: (i, 0)),
            _resident((1, D_MODEL)),
            _resident((D_MODEL, D_FF)),
            _resident((D_MODEL, D_FF)),
            _resident((D_FF, D_MODEL)),
        ],
        out_specs=pl.BlockSpec((ROW_TILE, D_MODEL), lambda i: (i, 0)),
        out_shape=jax.ShapeDtypeStruct((t, D_MODEL), F32),
        compiler_params=pltpu.CompilerParams(
            dimension_semantics=("arbitrary",), vmem_limit_bytes=VMEM_LIMIT),
        name="ffn",
    )(x2d, gain, w_gate, w_up, w_down)


def _ple_kernel(x_ref, p_ref, gain_ref, wpg_ref, wpp_ref, gfin_ref, o_ref):
    x = x_ref[...]
    h = _rms(x, gain_ref[...]).astype(BF16)
    gate = jax.nn.sigmoid(_dot(h, wpg_ref[...]))
    x = x + gate * _dot(p_ref[...].astype(BF16), wpp_ref[...])
    o_ref[...] = _rms(x, gfin_ref[...])


def _ple(x2d, p2d, gain, w_pg, w_pp, gain_final):
    t = x2d.shape[0]
    return pl.pallas_call(
        _ple_kernel,
        grid=(t // ROW_TILE,),
        in_specs=[
            pl.BlockSpec((ROW_TILE, D_MODEL), lambda i: (i, 0)),
            pl.BlockSpec((ROW_TILE, PLE_DIM), lambda i: (i, 0)),
            _resident((1, D_MODEL)),
            _resident((D_MODEL, D_MODEL)),
            _resident((PLE_DIM, D_MODEL)),
            _resident((1, D_MODEL)),
        ],
        out_specs=pl.BlockSpec((ROW_TILE, D_MODEL), lambda i: (i, 0)),
        out_shape=jax.ShapeDtypeStruct((t, D_MODEL), F32),
        compiler_params=pltpu.CompilerParams(
            dimension_semantics=("arbitrary",), vmem_limit_bytes=VMEM_LIMIT),
        name="ple",
    )(x2d, p2d, gain, w_pg, w_pp, gain_final)


def kernel(x, p, norm_mix, w_in, w_pool, pool_scale, w_branch_a, w_branch_b, w_out, norm_ffn,
           w_ffn_gate, w_ffn_up, w_ffn_down, norm_ple, w_ple_gate, w_ple_proj, norm_final):
    batch, seq, d_model = x.shape
    depth = w_in.shape[0]
    assert d_model == D_MODEL and seq % ROW_TILE == 0 and seq % Q_BLOCK == 0
    t = batch * seq
    x2d = x.reshape(t, D_MODEL)
    row = lambda v: v.reshape(1, -1).astype(F32)
    for i in range(depth):
        u, qkv, g = _proj(x2d, row(norm_mix[i]), w_in[i].astype(BF16))
        y_b = _attention(qkv, batch, seq)
        x2d = _merge(u, y_b, g, x2d, w_pool[i].astype(BF16), row(pool_scale[i]),
                     w_branch_a[i].astype(BF16), w_branch_b[i].astype(BF16),
                     w_out[i].astype(BF16), seq)
        x2d = _ffn(x2d, row(norm_ffn[i]), w_ffn_gate[i].astype(BF16),
                   w_ffn_up[i].astype(BF16), w_ffn_down[i].astype(BF16))
        last = i == depth - 1
        assert last, "DEPTH > 1 needs an un-normalised embedding kernel for the inner layers"
        x2d = _ple(x2d, p[i].reshape(t, PLE_DIM), row(norm_ple[i]), w_ple_gate[i].astype(BF16),
                   w_ple_proj[i].astype(BF16), row(norm_final))
    return x2d.reshape(batch, seq, D_MODEL)
```

```python
import functools

import jax
import jax.numpy as jnp
from jax import lax
from jax.experimental import pallas as pl
from jax.experimental.pallas import tpu as pltpu

D_MODEL = 1024
PLE_DIM = 256
D_POOL = D_MODEL // 2
POOL_WINDOWS = (2, 4, 8, 16)
POOL_GROUP_DIM = D_POOL // len(POOL_WINDOWS)
POOL_HALO = 16
SB_HEADS = 8
SB_HEAD_DIM = 64
D_SB = SB_HEADS * SB_HEAD_DIM
D_FF = ((8 * D_MODEL // 3 + 255) // 256) * 256
D_IN = D_POOL + 3 * D_SB + 2 * D_MODEL
RMS_EPS = 1e-6

LANES = 128
ROW_TILE = 512
Q_BLOCK = 256
K_BLOCK = 256
VMEM_LIMIT = 56 * 1024 * 1024
EXIT_LOG = -104.0

F32 = jnp.float32
BF16 = jnp.bfloat16


def _dot(a, b):
    return jnp.dot(a, b, preferred_element_type=F32)


def _rms(xf, gain_row):
    inv = lax.rsqrt(jnp.mean(xf * xf, axis=-1, keepdims=True) + RMS_EPS)
    return xf * inv * gain_row


def _resident(shape):
    zeros = (0,) * len(shape)
    return pl.BlockSpec(shape, lambda *_: zeros, pipeline_mode=pl.Buffered(1))


def _proj_kernel(x_ref, gain_ref, w_ref, u_ref, qkv_ref, g_ref):
    h = _rms(x_ref[...], gain_ref[...]).astype(BF16)
    u_ref[...] = _dot(h, w_ref[:, 0:D_POOL])
    q = _dot(h, w_ref[:, D_POOL:D_POOL + D_SB]) * (-1.0 / 8.0)
    qkv_ref[:, 0:D_SB] = q.astype(BF16)
    qkv_ref[:, D_SB:3 * D_SB] = _dot(h, w_ref[:, D_POOL + D_SB:D_POOL + 3 * D_SB]).astype(BF16)
    g_ref[...] = _dot(h, w_ref[:, D_POOL + 3 * D_SB:D_IN])


def _proj(x2d, gain, w_in):
    t = x2d.shape[0]
    return pl.pallas_call(
        _proj_kernel,
        grid=(t // ROW_TILE,),
        in_specs=[
            pl.BlockSpec((ROW_TILE, D_MODEL), lambda i: (i, 0)),
            _resident((1, D_MODEL)),
            _resident((D_MODEL, D_IN)),
        ],
        out_specs=[
            pl.BlockSpec((ROW_TILE, D_POOL), lambda i: (i, 0)),
            pl.BlockSpec((ROW_TILE, 3 * D_SB), lambda i: (i, 0)),
            pl.BlockSpec((ROW_TILE, 2 * D_MODEL), lambda i: (i, 0)),
        ],
        out_shape=[
            jax.ShapeDtypeStruct((t, D_POOL), F32),
            jax.ShapeDtypeStruct((t, 3 * D_SB), BF16),
            jax.ShapeDtypeStruct((t, 2 * D_MODEL), F32),
        ],
        compiler_params=pltpu.CompilerParams(
            dimension_semantics=("arbitrary",), vmem_limit_bytes=VMEM_LIMIT),
        name="proj",
    )(x2d, gain, w_in)


def _attn_kernel(q_ref, k_ref, v_ref, tri_ref, o_ref, acc_ref, carry_ref):
    qi = pl.program_id(2)
    rows = 2 * Q_BLOCK

    q2 = q_ref[...]
    lane = lax.broadcasted_iota(jnp.int32, (Q_BLOCK, LANES), 1)
    zero = jnp.zeros_like(q2)
    qs = jnp.concatenate([jnp.where(lane < SB_HEAD_DIM, q2, zero),
                          jnp.where(lane >= SB_HEAD_DIM, q2, zero)], axis=0)
    tri = tri_ref[...]

    acc_ref[...] = jnp.zeros_like(acc_ref)
    carry_ref[...] = jnp.zeros_like(carry_ref)

    def sweep(kb, diagonal):
        start = pl.multiple_of(kb * K_BLOCK, K_BLOCK)
        kblk = k_ref[pl.ds(start, K_BLOCK), :]
        vblk = v_ref[pl.ds(start, K_BLOCK), :]
        nz = lax.dot_general(qs, kblk, (((1,), (1,)), ((), ())),
                             preferred_element_type=F32)
        lf = jnp.minimum(nz, 0.0) - jnp.log(1.0 + jnp.exp(-jnp.abs(nz)))
        if diagonal:
            t_loc = lax.broadcasted_iota(jnp.int32, (rows, K_BLOCK), 0) & (Q_BLOCK - 1)
            s_loc = lax.broadcasted_iota(jnp.int32, (rows, K_BLOCK), 1)
            mask = s_loc < t_loc
            lf = jnp.where(mask, lf, 0.0)
        hi = lf.astype(BF16)
        lo = (lf - hi.astype(F32)).astype(BF16)
        suffix = _dot(hi, tri) + _dot(lo, tri)
        carry = carry_ref[...]
        a = jnp.exp((lf - nz) + suffix + carry)
        if diagonal:
            a = jnp.where(mask, a, 0.0)
        acc_ref[...] += _dot(a.astype(BF16), vblk)
        carry_ref[...] = carry + jnp.sum(lf, axis=1, keepdims=True)

    sweep(qi, True)

    def keep_going(c):
        kb, live = c
        return jnp.logical_and(kb >= 0, live > 0)

    def body(c):
        kb, _ = c
        sweep(kb, False)
        live = (jnp.max(carry_ref[...]) >= EXIT_LOG).astype(jnp.int32)
        return kb - 1, live

    lax.while_loop(keep_going, body, (qi - 1, jnp.int32(1)))

    acc = acc_ref[...]
    o_ref[...] = jnp.where(lane < SB_HEAD_DIM, acc[:Q_BLOCK], acc[Q_BLOCK:]).astype(o_ref.dtype)


def _attention(qkv, batch, seq):
    t = batch * seq
    n_q = seq // Q_BLOCK
    n_pair = D_SB // LANES
    jj = lax.broadcasted_iota(jnp.int32, (K_BLOCK, K_BLOCK), 0)
    ss = lax.broadcasted_iota(jnp.int32, (K_BLOCK, K_BLOCK), 1)
    tri = (jj > ss).astype(BF16)
    return pl.pallas_call(
        _attn_kernel,
        grid=(batch, n_pair, n_q),
        in_specs=[
            pl.BlockSpec((Q_BLOCK, LANES), lambda b, j, i: (b * n_q + i, j)),
            pl.BlockSpec((seq, LANES), lambda b, j, i: (b, n_pair + j)),
            pl.BlockSpec((seq, LANES), lambda b, j, i: (b, 2 * n_pair + j)),
            _resident((K_BLOCK, K_BLOCK)),
        ],
        out_specs=pl.BlockSpec((Q_BLOCK, LANES), lambda b, j, i: (b * n_q + i, j)),
        out_shape=jax.ShapeDtypeStruct((t, D_SB), BF16),
        scratch_shapes=[
            pltpu.VMEM((2 * Q_BLOCK, LANES), F32),
            pltpu.VMEM((2 * Q_BLOCK, 1), F32),
        ],
        compiler_params=pltpu.CompilerParams(
            dimension_semantics=("arbitrary", "arbitrary", "arbitrary"),
            vmem_limit_bytes=VMEM_LIMIT),
        name="attn",
    )(qkv, qkv, qkv, tri)


def _merge_kernel(seq, u_ref, halo_ref, yb_ref, g_ref, x_ref, wpool_ref, pscale_ref,
                  wa_ref, wb_ref, wout_ref, o_ref, ubuf_ref):
    i = pl.program_id(0)
    t0 = (i * ROW_TILE) % seq
    u = u_ref[...]
    halo = halo_ref[...]
    ubuf_ref[0:POOL_HALO, :] = jnp.where(t0 == 0, jnp.zeros_like(halo), halo)
    ubuf_ref[POOL_HALO:, :] = u
    pos = t0 + lax.broadcasted_iota(jnp.int32, (ROW_TILE, 1), 0)

    mixed = []
    for g, w in enumerate(POOL_WINDOWS):
        cols = slice(g * POOL_GROUP_DIM, (g + 1) * POOL_GROUP_DIM)
        win = u[:, cols]
        for j in range(1, w):
            win = win + ubuf_ref[POOL_HALO - j:POOL_HALO - j + ROW_TILE, cols]
        count = jnp.minimum(pos + 1, w).astype(F32)
        pooled = win / count - u[:, cols]
        mixed.append(_dot(pooled.astype(BF16), wpool_ref[g]))
    y_a = jnp.concatenate(mixed, axis=1) * pscale_ref[...]

    g_all = g_ref[...]
    merged = (jax.nn.sigmoid(g_all[:, :D_MODEL]) * _dot(y_a.astype(BF16), wa_ref[...])
              + jax.nn.sigmoid(g_all[:, D_MODEL:]) * _dot(yb_ref[...], wb_ref[...]))
    o_ref[...] = x_ref[...] + _dot(merged.astype(BF16), wout_ref[...])


def _merge(u, y_b, g, x2d, w_pool, pool_scale, w_a, w_b, w_out, seq):
    t = x2d.shape[0]
    halo_per_tile = ROW_TILE // POOL_HALO
    return pl.pallas_call(
        functools.partial(_merge_kernel, seq),
        grid=(t // ROW_TILE,),
        in_specs=[
            pl.BlockSpec((ROW_TILE, D_POOL), lambda i: (i, 0)),
            pl.BlockSpec((POOL_HALO, D_POOL), lambda i: (jnp.maximum(i * halo_per_tile - 1, 0), 0)),
            pl.BlockSpec((ROW_TILE, D_SB), lambda i: (i, 0)),
            pl.BlockSpec((ROW_TILE, 2 * D_MODEL), lambda i: (i, 0)),
            pl.BlockSpec((ROW_TILE, D_MODEL), lambda i: (i, 0)),
            _resident(w_pool.shape),
            _resident((1, D_POOL)),
            _resident((D_POOL, D_MODEL)),
            _resident((D_SB, D_MODEL)),
            _resident((D_MODEL, D_MODEL)),
        ],
        out_specs=pl.BlockSpec((ROW_TILE, D_MODEL), lambda i: (i, 0)),
        out_shape=jax.ShapeDtypeStruct((t, D_MODEL), F32),
        scratch_shapes=[pltpu.VMEM((ROW_TILE + POOL_HALO, D_POOL), F32)],
        compiler_params=pltpu.CompilerParams(
            dimension_semantics=("arbitrary",), vmem_limit_bytes=VMEM_LIMIT),
        name="merge",
    )(u, u, y_b, g, x2d, w_pool, pool_scale, w_a, w_b, w_out)


def _ffn_kernel(x_ref, gain_ref, wg_ref, wu_ref, wd_ref, o_ref):
    x = x_ref[...]
    h = _rms(x, gain_ref[...]).astype(BF16)
    act = jax.nn.silu(_dot(h, wg_ref[...])) * _dot(h, wu_ref[...])
    o_ref[...] = x + _dot(act.astype(BF16), wd_ref[...])


def _ffn(x2d, gain, w_gate, w_up, w_down):
    t = x2d.shape[0]
    return pl.pallas_call(
        _ffn_kernel,
        grid=(t // ROW_TILE,),
        in_specs=[
            pl.BlockSpec((ROW_TILE, D_MODEL), lambda i: (i, 0)),
            _resident((1, D_MODEL)),
            _resident((D_MODEL, D_FF)),
            _resident((D_MODEL, D_FF)),
            _resident((D_FF, D_MODEL)),
        ],
        out_specs=pl.BlockSpec((ROW_TILE, D_MODEL), lambda i: (i, 0)),
        out_shape=jax.ShapeDtypeStruct((t, D_MODEL), F32),
        compiler_params=pltpu.CompilerParams(
            dimension_semantics=("arbitrary",), vmem_limit_bytes=VMEM_LIMIT),
        name="ffn",
    )(x2d, gain, w_gate, w_up, w_down)


def _ple_kernel(x_ref, p_ref, gain_ref, wpg_ref, wpp_ref, gfin_ref, o_ref):
    x = x_ref[...]
    h = _rms(x, gain_ref[...]).astype(BF16)
    gate = jax.nn.sigmoid(_dot(h, wpg_ref[...]))
    x = x + gate * _dot(p_ref[...].astype(BF16), wpp_ref[...])
    o_ref[...] = _rms(x, gfin_ref[...])


def _ple(x2d, p2d, gain, w_pg, w_pp, gain_final):
    t = x2d.shape[0]
    return pl.pallas_call(
        _ple_kernel,
        grid=(t // ROW_TILE,),
        in_specs=[
            pl.BlockSpec((ROW_TILE, D_MODEL), lambda i: (i, 0)),
            pl.BlockSpec((ROW_TILE, PLE_DIM), lambda i: (i, 0)),
            _resident((1, D_MODEL)),
            _resident((D_MODEL, D_MODEL)),
            _resident((PLE_DIM, D_MODEL)),
            _resident((1, D_MODEL)),
        ],
        out_specs=pl.BlockSpec((ROW_TILE, D_MODEL), lambda i: (i, 0)),
        out_shape=jax.ShapeDtypeStruct((t, D_MODEL), F32),
        compiler_params=pltpu.CompilerParams(
            dimension_semantics=("arbitrary",), vmem_limit_bytes=VMEM_LIMIT),
        name="ple",
    )(x2d, p2d, gain, w_pg, w_pp, gain_final)


def kernel(x, p, norm_mix, w_in, w_pool, pool_scale, w_branch_a, w_branch_b, w_out, norm_ffn,
           w_ffn_gate, w_ffn_up, w_ffn_down, norm_ple, w_ple_gate, w_ple_proj, norm_final):
    batch, seq, d_model = x.shape
    depth = w_in.shape[0]
    assert d_model == D_MODEL and seq % ROW_TILE == 0 and seq % Q_BLOCK == 0
    t = batch * seq
    x2d = x.reshape(t, D_MODEL)
    row = lambda v: v.reshape(1, -1).astype(F32)
    for i in range(depth):
        u, qkv, g = _proj(x2d, row(norm_mix[i]), w_in[i].astype(BF16))
        y_b = _attention(qkv, batch, seq)
        x2d = _merge(u, y_b, g, x2d, w_pool[i].astype(BF16), row(pool_scale[i]),
                     w_branch_a[i].astype(BF16), w_branch_b[i].astype(BF16),
                     w_out[i].astype(BF16), seq)
        x2d = _ffn(x2d, row(norm_ffn[i]), w_ffn_gate[i].astype(BF16),
                   w_ffn_up[i].astype(BF16), w_ffn_down[i].astype(BF16))
        last = i == depth - 1
        assert last, "DEPTH > 1 needs an un-normalised embedding kernel for the inner layers"
        x2d = _ple(x2d, p[i].reshape(t, PLE_DIM), row(norm_ple[i]), w_ple_gate[i].astype(BF16),
                   w_ple_proj[i].astype(BF16), row(norm_final))
    return x2d.reshape(batch, seq, D_MODEL)
```

```python
import functools

import jax
import jax.numpy as jnp
from jax import lax
from jax.experimental import pallas as pl
from jax.experimental.pallas import tpu as pltpu

D_MODEL = 1024
PLE_DIM = 256
D_POOL = D_MODEL // 2
POOL_WINDOWS = (2, 4, 8, 16)
POOL_GROUP_DIM = D_POOL // len(POOL_WINDOWS)
POOL_HALO = 16
SB_HEADS = 8
SB_HEAD_DIM = 64
D_SB = SB_HEADS * SB_HEAD_DIM
D_FF = ((8 * D_MODEL // 3 + 255) // 256) * 256
D_IN = D_POOL + 3 * D_SB + 2 * D_MODEL
RMS_EPS = 1e-6

LANES = 128
PROJ_TILE = 512
POST_TILE = 512
Q_BLOCK = 256
K_BLOCK = 256
VMEM_LIMIT = 56 * 1024 * 1024
EXIT_LOG = -104.0

LOG2E = 1.4426950408889634

F32 = jnp.float32
BF16 = jnp.bfloat16


def _dot(a, b):
    return jnp.dot(a, b, preferred_element_type=F32)


def _rms(xf, gain_row):
    inv = lax.rsqrt(jnp.mean(xf * xf, axis=-1, keepdims=True) + RMS_EPS)
    return xf * inv * gain_row


def _resident(shape):
    zeros = (0,) * len(shape)
    return pl.BlockSpec(shape, lambda *_: zeros, pipeline_mode=pl.Buffered(1))


def _rows(tile, width):
    return pl.BlockSpec((tile, width), lambda i: (i, 0))


def _proj_kernel(x_ref, gain_ref, w_ref, u_ref, qkv_ref, g_ref):
    h = _rms(x_ref[...], gain_ref[...]).astype(BF16)
    u_ref[...] = _dot(h, w_ref[:, 0:D_POOL])
    q = _dot(h, w_ref[:, D_POOL:D_POOL + D_SB]) * (-1.0 / 8.0)
    qkv_ref[:, 0:D_SB] = q.astype(BF16)
    qkv_ref[:, D_SB:3 * D_SB] = _dot(h, w_ref[:, D_POOL + D_SB:D_POOL + 3 * D_SB]).astype(BF16)
    g_ref[...] = _dot(h, w_ref[:, D_POOL + 3 * D_SB:D_IN])


def _proj(x2d, gain, w_in):
    t = x2d.shape[0]
    return pl.pallas_call(
        _proj_kernel,
        grid=(t // PROJ_TILE,),
        in_specs=[
            _rows(PROJ_TILE, D_MODEL),
            _resident((1, D_MODEL)),
            _resident((D_MODEL, D_IN)),
        ],
        out_specs=[
            _rows(PROJ_TILE, D_POOL),
            _rows(PROJ_TILE, 3 * D_SB),
            _rows(PROJ_TILE, 2 * D_MODEL),
        ],
        out_shape=[
            jax.ShapeDtypeStruct((t, D_POOL), F32),
            jax.ShapeDtypeStruct((t, 3 * D_SB), BF16),
            jax.ShapeDtypeStruct((t, 2 * D_MODEL), F32),
        ],
        compiler_params=pltpu.CompilerParams(
            dimension_semantics=("arbitrary",), vmem_limit_bytes=VMEM_LIMIT),
        name="proj",
    )(x2d, gain, w_in)


def _sweep(qs, k_ref, v_ref, tri, kb, carry, diagonal):
    rows = qs.shape[0]
    start = pl.multiple_of(kb * K_BLOCK, K_BLOCK)
    kblk = k_ref[pl.ds(start, K_BLOCK), :]
    vblk = v_ref[pl.ds(start, K_BLOCK), :]
    nz = lax.dot_general(qs, kblk, (((1,), (1,)), ((), ())),
                         preferred_element_type=F32)
    lf = jnp.minimum(nz, 0.0) - jnp.log(1.0 + jnp.exp2(jnp.abs(nz) * (-LOG2E)))
    if diagonal:
        t_loc = lax.broadcasted_iota(jnp.int32, (rows, K_BLOCK), 0) & (Q_BLOCK - 1)
        s_loc = lax.broadcasted_iota(jnp.int32, (rows, K_BLOCK), 1)
        mask = s_loc < t_loc
        lf = jnp.where(mask, lf, 0.0)
    hi = lf.astype(BF16)
    lo = (lf - hi.astype(F32)).astype(BF16)
    suffix = _dot(jnp.concatenate([hi, lo], axis=1), tri)
    a = jnp.exp((lf - nz) + suffix)
    if diagonal:
        a = jnp.where(mask, a, 0.0)
    pv = _dot(a.astype(BF16), vblk)
    if carry is not None:
        pv = pv * jnp.exp(carry)
    return pv, jnp.sum(lf, axis=1, keepdims=True)


def _attn_kernel(q_ref, k_ref, v_ref, tri_ref, o_ref, acc_ref, carry_ref):
    qi = pl.program_id(2)

    q2 = q_ref[...]
    lane = lax.broadcasted_iota(jnp.int32, (Q_BLOCK, LANES), 1)
    zero = jnp.zeros_like(q2)
    qs = jnp.concatenate([jnp.where(lane < SB_HEAD_DIM, q2, zero),
                          jnp.where(lane >= SB_HEAD_DIM, q2, zero)], axis=0)
    tri = tri_ref[...]

    has_prev = qi > 0
    pv_d, tot_d = _sweep(qs, k_ref, v_ref, tri, qi, None, True)
    pv_p, tot_p = _sweep(qs, k_ref, v_ref, tri, jnp.maximum(qi - 1, 0), tot_d, False)
    acc_ref[...] = pv_d + jnp.where(has_prev, pv_p, 0.0)
    carry_ref[...] = tot_d + jnp.where(has_prev, tot_p, 0.0)

    def live():
        return (jnp.max(carry_ref[...]) >= EXIT_LOG).astype(jnp.int32)

    def keep_going(c):
        kb, flag = c
        return jnp.logical_and(kb >= 0, flag > 0)

    def body(c):
        kb, _ = c
        carry = carry_ref[...]
        pv, tot = _sweep(qs, k_ref, v_ref, tri, kb, carry, False)
        acc_ref[...] += pv
        carry_ref[...] = carry + tot
        return kb - 1, live()

    lax.while_loop(keep_going, body, (qi - 2, live()))

    acc = acc_ref[...]
    o_ref[...] = jnp.where(lane < SB_HEAD_DIM, acc[:Q_BLOCK], acc[Q_BLOCK:]).astype(o_ref.dtype)


def _attention(qkv, batch, seq):
    t = batch * seq
    n_q = seq // Q_BLOCK
    n_pair = D_SB // LANES
    jj = lax.broadcasted_iota(jnp.int32, (2 * K_BLOCK, K_BLOCK), 0) & (K_BLOCK - 1)
    ss = lax.broadcasted_iota(jnp.int32, (2 * K_BLOCK, K_BLOCK), 1)
    tri = (jj > ss).astype(BF16)
    return pl.pallas_call(
        _attn_kernel,
        grid=(batch, n_pair, n_q),
        in_specs=[
            pl.BlockSpec((Q_BLOCK, LANES), lambda b, j, i: (b * n_q + i, j)),
            pl.BlockSpec((seq, LANES), lambda b, j, i: (b, n_pair + j)),
            pl.BlockSpec((seq, LANES), lambda b, j, i: (b, 2 * n_pair + j)),
            _resident((2 * K_BLOCK, K_BLOCK)),
        ],
        out_specs=pl.BlockSpec((Q_BLOCK, LANES), lambda b, j, i: (b * n_q + i, j)),
        out_shape=jax.ShapeDtypeStruct((t, D_SB), BF16),
        scratch_shapes=[
            pltpu.VMEM((2 * Q_BLOCK, LANES), F32),
            pltpu.VMEM((2 * Q_BLOCK, 1), F32),
        ],
        compiler_params=pltpu.CompilerParams(
            dimension_semantics=("arbitrary", "arbitrary", "arbitrary"),
            vmem_limit_bytes=VMEM_LIMIT),
        name="attn",
    )(qkv, qkv, qkv, tri)


def _post_kernel(seq, u_ref, halo_ref, yb_ref, g_ref, x_ref, p_ref,
                 wpool_ref, pscale_ref, wa_ref, wb_ref, wout_ref,
                 gffn_ref, wgate_ref, wup_ref, wdown_ref,
                 gple_ref, wpg_ref, wpp_ref, gfin_ref, o_ref, ubuf_ref):
    i = pl.program_id(0)
    t0 = (i * POST_TILE) % seq
    u = u_ref[...]
    halo = halo_ref[...]
    ubuf_ref[0:POOL_HALO, :] = jnp.where(t0 == 0, jnp.zeros_like(halo), halo)
    ubuf_ref[POOL_HALO:, :] = u
    pos = t0 + lax.broadcasted_iota(jnp.int32, (POST_TILE, 1), 0)
    mixed = []
    for g, w in enumerate(POOL_WINDOWS):
        cols = slice(g * POOL_GROUP_DIM, (g + 1) * POOL_GROUP_DIM)
        win = u[:, cols]
        for j in range(1, w):
            win = win + ubuf_ref[POOL_HALO - j:POOL_HALO - j + POST_TILE, cols]
        count = jnp.minimum(pos + 1, w).astype(F32)
        pooled = win / count - u[:, cols]
        mixed.append(_dot(pooled.astype(BF16), wpool_ref[g]))
    y_a = jnp.concatenate(mixed, axis=1) * pscale_ref[...]

    g_all = g_ref[...]
    merged = (jax.nn.sigmoid(g_all[:, :D_MODEL]) * _dot(y_a.astype(BF16), wa_ref[...])
              + jax.nn.sigmoid(g_all[:, D_MODEL:]) * _dot(yb_ref[...], wb_ref[...]))
    x = x_ref[...] + _dot(merged.astype(BF16), wout_ref[...])

    h = _rms(x, gffn_ref[...]).astype(BF16)
    act = jax.nn.silu(_dot(h, wgate_ref[...])) * _dot(h, wup_ref[...])
    x = x + _dot(act.astype(BF16), wdown_ref[...])

    h = _rms(x, gple_ref[...]).astype(BF16)
    gate = jax.nn.sigmoid(_dot(h, wpg_ref[...]))
    x = x + gate * _dot(p_ref[...].astype(BF16), wpp_ref[...])
    o_ref[...] = _rms(x, gfin_ref[...])


def _post(u, y_b, g, x2d, p2d, seq, w_pool, pool_scale, w_a, w_b, w_out,
          gain_ffn, w_gate, w_up, w_down, gain_ple, w_pg, w_pp, gain_final):
    t = x2d.shape[0]
    halo_per_tile = POST_TILE // POOL_HALO
    weights = (w_pool, pool_scale, w_a, w_b, w_out, gain_ffn, w_gate, w_up, w_down,
               gain_ple, w_pg, w_pp, gain_final)
    return pl.pallas_call(
        functools.partial(_post_kernel, seq),
        grid=(t // POST_TILE,),
        in_specs=[
            _rows(POST_TILE, D_POOL),
            pl.BlockSpec((POOL_HALO, D_POOL), lambda i: (jnp.maximum(i * halo_per_tile - 1, 0), 0)),
            _rows(POST_TILE, D_SB),
            _rows(POST_TILE, 2 * D_MODEL),
            _rows(POST_TILE, D_MODEL),
            _rows(POST_TILE, PLE_DIM),
        ] + [_resident(w.shape) for w in weights],
        out_specs=_rows(POST_TILE, D_MODEL),
        out_shape=jax.ShapeDtypeStruct((t, D_MODEL), F32),
        scratch_shapes=[pltpu.VMEM((POST_TILE + POOL_HALO, D_POOL), F32)],
        compiler_params=pltpu.CompilerParams(
            dimension_semantics=("arbitrary",), vmem_limit_bytes=VMEM_LIMIT),
        name="post",
    )(u, u, y_b, g, x2d, p2d, *weights)


def kernel(x, p, norm_mix, w_in, w_pool, pool_scale, w_branch_a, w_branch_b, w_out, norm_ffn,
           w_ffn_gate, w_ffn_up, w_ffn_down, norm_ple, w_ple_gate, w_ple_proj, norm_final):
    batch, seq, d_model = x.shape
    assert w_in.shape[0] == 1 and d_model == D_MODEL
    assert seq % PROJ_TILE == 0 and seq % POST_TILE == 0 and seq % Q_BLOCK == 0
    t = batch * seq
    x2d = x.reshape(t, D_MODEL)
    row = lambda v: v.reshape(1, -1).astype(F32)
    bf = lambda w: w[0].astype(BF16)
    u, qkv, g = _proj(x2d, row(norm_mix[0]), bf(w_in))
    y_b = _attention(qkv, batch, seq)
    out = _post(u, y_b, g, x2d, p[0].reshape(t, PLE_DIM), seq,
                bf(w_pool), row(pool_scale[0]), bf(w_branch_a), bf(w_branch_b), bf(w_out),
                row(norm_ffn[0]), bf(w_ffn_gate), bf(w_ffn_up), bf(w_ffn_down),
                row(norm_ple[0]), bf(w_ple_gate), bf(w_ple_proj), row(norm_final))
    return out.reshape(batch, seq, D_MODEL)
```

```python
import functools

import jax
import jax.numpy as jnp
from jax import lax
from jax.experimental import pallas as pl
from jax.experimental.pallas import tpu as pltpu

D_MODEL = 1024
PLE_DIM = 256
D_POOL = D_MODEL // 2
POOL_WINDOWS = (2, 4, 8, 16)
POOL_GROUP_DIM = D_POOL // len(POOL_WINDOWS)
POOL_HALO = 16
SB_HEADS = 8
SB_HEAD_DIM = 64
D_SB = SB_HEADS * SB_HEAD_DIM
D_FF = ((8 * D_MODEL // 3 + 255) // 256) * 256
D_IN = D_POOL + 3 * D_SB + 2 * D_MODEL
RMS_EPS = 1e-6

LANES = 128
PROJ_TILE = 512
POST_TILE = 512
Q_BLOCK = 256
K_BLOCK = 256
VMEM_LIMIT = 56 * 1024 * 1024
EXIT_LOG = -104.0

LOG2E = 1.4426950408889634

F32 = jnp.float32
BF16 = jnp.bfloat16


def _dot(a, b):
    return jnp.dot(a, b, preferred_element_type=F32)


def _rms(xf, gain_row):
    inv = lax.rsqrt(jnp.mean(xf * xf, axis=-1, keepdims=True) + RMS_EPS)
    return xf * inv * gain_row


def _resident(shape):
    zeros = (0,) * len(shape)
    return pl.BlockSpec(shape, lambda *_: zeros, pipeline_mode=pl.Buffered(1))


def _rows(tile, width):
    return pl.BlockSpec((tile, width), lambda i: (i, 0))


def _proj_kernel(x_ref, gain_ref, w_ref, u_ref, qkv_ref, g_ref):
    h = _rms(x_ref[...], gain_ref[...]).astype(BF16)
    u_ref[...] = _dot(h, w_ref[:, 0:D_POOL])
    q = _dot(h, w_ref[:, D_POOL:D_POOL + D_SB]) * (-1.0 / 8.0)
    qkv_ref[:, 0:D_SB] = q.astype(BF16)
    qkv_ref[:, D_SB:3 * D_SB] = _dot(h, w_ref[:, D_POOL + D_SB:D_POOL + 3 * D_SB]).astype(BF16)
    g_ref[...] = _dot(h, w_ref[:, D_POOL + 3 * D_SB:D_IN])


def _proj(x2d, gain, w_in):
    t = x2d.shape[0]
    return pl.pallas_call(
        _proj_kernel,
        grid=(t // PROJ_TILE,),
        in_specs=[
            _rows(PROJ_TILE, D_MODEL),
            _resident((1, D_MODEL)),
            _resident((D_MODEL, D_IN)),
        ],
        out_specs=[
            _rows(PROJ_TILE, D_POOL),
            _rows(PROJ_TILE, 3 * D_SB),
            _rows(PROJ_TILE, 2 * D_MODEL),
        ],
        out_shape=[
            jax.ShapeDtypeStruct((t, D_POOL), F32),
            jax.ShapeDtypeStruct((t, 3 * D_SB), BF16),
            jax.ShapeDtypeStruct((t, 2 * D_MODEL), F32),
        ],
        compiler_params=pltpu.CompilerParams(
            dimension_semantics=("arbitrary",), vmem_limit_bytes=VMEM_LIMIT),
        name="proj",
    )(x2d, gain, w_in)


def _sweep(qs, k_ref, v_ref, tri, kb, carry, diagonal):
    rows = qs.shape[0]
    start = pl.multiple_of(kb * K_BLOCK, K_BLOCK)
    kblk = k_ref[pl.ds(start, K_BLOCK), :]
    vblk = v_ref[pl.ds(start, K_BLOCK), :]
    nz = lax.dot_general(qs, kblk, (((1,), (1,)), ((), ())),
                         preferred_element_type=F32)
    lf = jnp.minimum(nz, 0.0) - jnp.log(1.0 + jnp.exp2(jnp.abs(nz) * (-LOG2E)))
    if diagonal:
        t_loc = lax.broadcasted_iota(jnp.int32, (rows, K_BLOCK), 0) & (Q_BLOCK - 1)
        s_loc = lax.broadcasted_iota(jnp.int32, (rows, K_BLOCK), 1)
        mask = s_loc < t_loc
        lf = jnp.where(mask, lf, 0.0)
    suffix = _dot(lf.astype(BF16), tri)
    a = jnp.exp((lf - nz) + suffix)
    if diagonal:
        a = jnp.where(mask, a, 0.0)
    pv = _dot(a.astype(BF16), vblk)
    if carry is not None:
        pv = pv * jnp.exp(carry)
    return pv, jnp.sum(lf, axis=1, keepdims=True)


def _attn_kernel(q_ref, k_ref, v_ref, tri_ref, o_ref, acc_ref, carry_ref):
    qi = pl.program_id(2)

    q2 = q_ref[...]
    lane = lax.broadcasted_iota(jnp.int32, (Q_BLOCK, LANES), 1)
    zero = jnp.zeros_like(q2)
    qs = jnp.concatenate([jnp.where(lane < SB_HEAD_DIM, q2, zero),
                          jnp.where(lane >= SB_HEAD_DIM, q2, zero)], axis=0)
    tri = tri_ref[...]

    has_prev = qi > 0
    pv_d, tot_d = _sweep(qs, k_ref, v_ref, tri, qi, None, True)
    pv_p, tot_p = _sweep(qs, k_ref, v_ref, tri, jnp.maximum(qi - 1, 0), tot_d, False)
    acc_ref[...] = pv_d + jnp.where(has_prev, pv_p, 0.0)
    carry_ref[...] = tot_d + jnp.where(has_prev, tot_p, 0.0)

    def live():
        return (jnp.max(carry_ref[...]) >= EXIT_LOG).astype(jnp.int32)

    def keep_going(c):
        kb, flag = c
        return jnp.logical_and(kb >= 0, flag > 0)

    def body(c):
        kb, _ = c
        carry = carry_ref[...]
        pv, tot = _sweep(qs, k_ref, v_ref, tri, kb, carry, False)
        acc_ref[...] += pv
        carry_ref[...] = carry + tot
        return kb - 1, live()

    lax.while_loop(keep_going, body, (qi - 2, live()))

    acc = acc_ref[...]
    o_ref[...] = jnp.where(lane < SB_HEAD_DIM, acc[:Q_BLOCK], acc[Q_BLOCK:]).astype(o_ref.dtype)


def _attention(qkv, batch, seq):
    t = batch * seq
    n_q = seq // Q_BLOCK
    n_pair = D_SB // LANES
    jj = lax.broadcasted_iota(jnp.int32, (K_BLOCK, K_BLOCK), 0)
    ss = lax.broadcasted_iota(jnp.int32, (K_BLOCK, K_BLOCK), 1)
    tri = (jj > ss).astype(BF16)
    return pl.pallas_call(
        _attn_kernel,
        grid=(batch, n_pair, n_q),
        in_specs=[
            pl.BlockSpec((Q_BLOCK, LANES), lambda b, j, i: (b * n_q + i, j)),
            pl.BlockSpec((seq, LANES), lambda b, j, i: (b, n_pair + j)),
            pl.BlockSpec((seq, LANES), lambda b, j, i: (b, 2 * n_pair + j)),
            _resident((K_BLOCK, K_BLOCK)),
        ],
        out_specs=pl.BlockSpec((Q_BLOCK, LANES), lambda b, j, i: (b * n_q + i, j)),
        out_shape=jax.ShapeDtypeStruct((t, D_SB), BF16),
        scratch_shapes=[
            pltpu.VMEM((2 * Q_BLOCK, LANES), F32),
            pltpu.VMEM((2 * Q_BLOCK, 1), F32),
        ],
        compiler_params=pltpu.CompilerParams(
            dimension_semantics=("arbitrary", "arbitrary", "arbitrary"),
            vmem_limit_bytes=VMEM_LIMIT),
        name="attn",
    )(qkv, qkv, qkv, tri)


def _post_kernel(seq, u_ref, halo_ref, yb_ref, g_ref, x_ref, p_ref,
                 wpool_ref, pscale_ref, wa_ref, wb_ref, wout_ref,
                 gffn_ref, wgate_ref, wup_ref, wdown_ref,
                 gple_ref, wpg_ref, wpp_ref, gfin_ref, o_ref, ubuf_ref):
    i = pl.program_id(0)
    t0 = (i * POST_TILE) % seq
    u = u_ref[...]
    halo = halo_ref[...]
    ubuf_ref[0:POOL_HALO, :] = jnp.where(t0 == 0, jnp.zeros_like(halo), halo)
    ubuf_ref[POOL_HALO:, :] = u
    pos = t0 + lax.broadcasted_iota(jnp.int32, (POST_TILE, 1), 0)
    mixed = []
    for g, w in enumerate(POOL_WINDOWS):
        cols = slice(g * POOL_GROUP_DIM, (g + 1) * POOL_GROUP_DIM)
        win = u[:, cols]
        for j in range(1, w):
            win = win + ubuf_ref[POOL_HALO - j:POOL_HALO - j + POST_TILE, cols]
        count = jnp.minimum(pos + 1, w).astype(F32)
        pooled = win / count - u[:, cols]
        mixed.append(_dot(pooled.astype(BF16), wpool_ref[g]))
    y_a = jnp.concatenate(mixed, axis=1) * pscale_ref[...]

    g_all = g_ref[...]
    merged = (jax.nn.sigmoid(g_all[:, :D_MODEL]) * _dot(y_a.astype(BF16), wa_ref[...])
              + jax.nn.sigmoid(g_all[:, D_MODEL:]) * _dot(yb_ref[...], wb_ref[...]))
    x = x_ref[...] + _dot(merged.astype(BF16), wout_ref[...])

    h = _rms(x, gffn_ref[...]).astype(BF16)
    act = jax.nn.silu(_dot(h, wgate_ref[...])) * _dot(h, wup_ref[...])
    x = x + _dot(act.astype(BF16), wdown_ref[...])

    h = _rms(x, gple_ref[...]).astype(BF16)
    gate = jax.nn.sigmoid(_dot(h, wpg_ref[...]))
    x = x + gate * _dot(p_ref[...].astype(BF16), wpp_ref[...])
    o_ref[...] = _rms(x, gfin_ref[...])


def _post(u, y_b, g, x2d, p2d, seq, w_pool, pool_scale, w_a, w_b, w_out,
          gain_ffn, w_gate, w_up, w_down, gain_ple, w_pg, w_pp, gain_final):
    t = x2d.shape[0]
    halo_per_tile = POST_TILE // POOL_HALO
    weights = (w_pool, pool_scale, w_a, w_b, w_out, gain_ffn, w_gate, w_up, w_down,
               gain_ple, w_pg, w_pp, gain_final)
    return pl.pallas_call(
        functools.partial(_post_kernel, seq),
        grid=(t // POST_TILE,),
        in_specs=[
            _rows(POST_TILE, D_POOL),
            pl.BlockSpec((POOL_HALO, D_POOL), lambda i: (jnp.maximum(i * halo_per_tile - 1, 0), 0)),
            _rows(POST_TILE, D_SB),
            _rows(POST_TILE, 2 * D_MODEL),
            _rows(POST_TILE, D_MODEL),
            _rows(POST_TILE, PLE_DIM),
        ] + [_resident(w.shape) for w in weights],
        out_specs=_rows(POST_TILE, D_MODEL),
        out_shape=jax.ShapeDtypeStruct((t, D_MODEL), F32),
        scratch_shapes=[pltpu.VMEM((POST_TILE + POOL_HALO, D_POOL), F32)],
        compiler_params=pltpu.CompilerParams(
            dimension_semantics=("arbitrary",), vmem_limit_bytes=VMEM_LIMIT),
        name="post",
    )(u, u, y_b, g, x2d, p2d, *weights)


def kernel(x, p, norm_mix, w_in, w_pool, pool_scale, w_branch_a, w_branch_b, w_out, norm_ffn,
           w_ffn_gate, w_ffn_up, w_ffn_down, norm_ple, w_ple_gate, w_ple_proj, norm_final):
    batch, seq, d_model = x.shape
    assert w_in.shape[0] == 1 and d_model == D_MODEL
    assert seq % PROJ_TILE == 0 and seq % POST_TILE == 0 and seq % Q_BLOCK == 0
    t = batch * seq
    x2d = x.reshape(t, D_MODEL)
    row = lambda v: v.reshape(1, -1).astype(F32)
    bf = lambda w: w[0].astype(BF16)
    u, qkv, g = _proj(x2d, row(norm_mix[0]), bf(w_in))
    y_b = _attention(qkv, batch, seq)
    out = _post(u, y_b, g, x2d, p[0].reshape(t, PLE_DIM), seq,
                bf(w_pool), row(pool_scale[0]), bf(w_branch_a), bf(w_branch_b), bf(w_out),
                row(norm_ffn[0]), bf(w_ffn_gate), bf(w_ffn_up), bf(w_ffn_down),
                row(norm_ple[0]), bf(w_ple_gate), bf(w_ple_proj), row(norm_final))
    return out.reshape(batch, seq, D_MODEL)
```

```python
import functools

import jax
import jax.numpy as jnp
from jax import lax
from jax.experimental import pallas as pl
from jax.experimental.pallas import tpu as pltpu

D_MODEL = 1024
PLE_DIM = 256
D_POOL = D_MODEL // 2
POOL_WINDOWS = (2, 4, 8, 16)
POOL_GROUP_DIM = D_POOL // len(POOL_WINDOWS)
POOL_HALO = 16
SB_HEADS = 8
SB_HEAD_DIM = 64
D_SB = SB_HEADS * SB_HEAD_DIM
D_FF = ((8 * D_MODEL // 3 + 255) // 256) * 256
D_IN = D_POOL + 3 * D_SB + 2 * D_MODEL
RMS_EPS = 1e-6

LANES = 128
PROJ_TILE = 512
POST_TILE = 512
Q_BLOCK = 256
K_BLOCK = 256
PAIRS_PER_STEP = 4
VMEM_LIMIT = 56 * 1024 * 1024
EXIT_LOG = -104.0

LOG2E = 1.4426950408889634

F32 = jnp.float32
BF16 = jnp.bfloat16


def _dot(a, b):
    return jnp.dot(a, b, preferred_element_type=F32)


def _rms(xf, gain_row):
    inv = lax.rsqrt(jnp.mean(xf * xf, axis=-1, keepdims=True) + RMS_EPS)
    return xf * inv * gain_row


def _resident(shape):
    zeros = (0,) * len(shape)
    return pl.BlockSpec(shape, lambda *_: zeros, pipeline_mode=pl.Buffered(1))


def _rows(tile, width):
    return pl.BlockSpec((tile, width), lambda i: (i, 0))


def _proj_kernel(x_ref, gain_ref, w_ref, u_ref, qkv_ref, g_ref):
    h = _rms(x_ref[...], gain_ref[...]).astype(BF16)
    u_ref[...] = _dot(h, w_ref[:, 0:D_POOL])
    q = _dot(h, w_ref[:, D_POOL:D_POOL + D_SB]) * (-1.0 / 8.0)
    qkv_ref[:, 0:D_SB] = q.astype(BF16)
    qkv_ref[:, D_SB:3 * D_SB] = _dot(h, w_ref[:, D_POOL + D_SB:D_POOL + 3 * D_SB]).astype(BF16)
    g_ref[...] = _dot(h, w_ref[:, D_POOL + 3 * D_SB:D_IN])


def _proj(x2d, gain, w_in):
    t = x2d.shape[0]
    return pl.pallas_call(
        _proj_kernel,
        grid=(t // PROJ_TILE,),
        in_specs=[
            _rows(PROJ_TILE, D_MODEL),
            _resident((1, D_MODEL)),
            _resident((D_MODEL, D_IN)),
        ],
        out_specs=[
            _rows(PROJ_TILE, D_POOL),
            _rows(PROJ_TILE, 3 * D_SB),
            _rows(PROJ_TILE, 2 * D_MODEL),
        ],
        out_shape=[
            jax.ShapeDtypeStruct((t, D_POOL), F32),
            jax.ShapeDtypeStruct((t, 3 * D_SB), BF16),
            jax.ShapeDtypeStruct((t, 2 * D_MODEL), F32),
        ],
        compiler_params=pltpu.CompilerParams(
            dimension_semantics=("arbitrary",), vmem_limit_bytes=VMEM_LIMIT),
        name="proj",
    )(x2d, gain, w_in)


def _sweep(qs, k_ref, v_ref, cols, tri, kb, carry, diagonal):
    rows = qs.shape[0]
    start = pl.multiple_of(kb * K_BLOCK, K_BLOCK)
    kblk = k_ref[pl.ds(start, K_BLOCK), cols]
    vblk = v_ref[pl.ds(start, K_BLOCK), cols]
    nz = lax.dot_general(qs, kblk, (((1,), (1,)), ((), ())),
                         preferred_element_type=F32)
    lf = jnp.minimum(nz, 0.0) - jnp.log(1.0 + jnp.exp2(jnp.abs(nz) * (-LOG2E)))
    if diagonal:
        t_loc = lax.broadcasted_iota(jnp.int32, (rows, K_BLOCK), 0) & (Q_BLOCK - 1)
        s_loc = lax.broadcasted_iota(jnp.int32, (rows, K_BLOCK), 1)
        mask = s_loc < t_loc
        lf = jnp.where(mask, lf, 0.0)
    suffix = _dot(lf.astype(BF16), tri)
    a = jnp.exp((lf - nz) + suffix)
    if diagonal:
        a = jnp.where(mask, a, 0.0)
    pv = _dot(a.astype(BF16), vblk)
    if carry is not None:
        pv = pv * jnp.exp(carry)
    return pv, jnp.sum(lf, axis=1, keepdims=True)


def _attn_kernel(q_ref, k_ref, v_ref, tri_ref, o_ref, acc_ref, carry_ref):
    qi = pl.program_id(2)
    lane = lax.broadcasted_iota(jnp.int32, (Q_BLOCK, LANES), 1)
    tri = tri_ref[...]
    pairs = [slice(pr * LANES, (pr + 1) * LANES) for pr in range(PAIRS_PER_STEP)]

    def stacked_q(cols):
        q2 = q_ref[:, cols]
        zero = jnp.zeros_like(q2)
        return jnp.concatenate([jnp.where(lane < SB_HEAD_DIM, q2, zero),
                                jnp.where(lane >= SB_HEAD_DIM, q2, zero)], axis=0)

    has_prev = qi > 0
    for pr, cols in enumerate(pairs):
        qs = stacked_q(cols)
        pv_d, tot_d = _sweep(qs, k_ref, v_ref, cols, tri, qi, None, True)
        pv_p, tot_p = _sweep(qs, k_ref, v_ref, cols, tri, jnp.maximum(qi - 1, 0), tot_d, False)
        acc_ref[pr] = pv_d + jnp.where(has_prev, pv_p, 0.0)
        carry_ref[pr] = tot_d + jnp.where(has_prev, tot_p, 0.0)

    def live():
        return (jnp.max(carry_ref[...]) >= EXIT_LOG).astype(jnp.int32)

    def keep_going(c):
        kb, flag = c
        return jnp.logical_and(kb >= 0, flag > 0)

    def body(c):
        kb, _ = c
        for pr, cols in enumerate(pairs):
            carry = carry_ref[pr]
            pv, tot = _sweep(stacked_q(cols), k_ref, v_ref, cols, tri, kb, carry, False)
            acc_ref[pr] += pv
            carry_ref[pr] = carry + tot
        return kb - 1, live()

    lax.while_loop(keep_going, body, (qi - 2, live()))

    for pr, cols in enumerate(pairs):
        acc = acc_ref[pr]
        o_ref[:, cols] = jnp.where(lane < SB_HEAD_DIM, acc[:Q_BLOCK], acc[Q_BLOCK:]).astype(o_ref.dtype)


def _attention(qkv, batch, seq):
    t = batch * seq
    n_q = seq // Q_BLOCK
    width = PAIRS_PER_STEP * LANES
    n_grp = D_SB // width
    jj = lax.broadcasted_iota(jnp.int32, (K_BLOCK, K_BLOCK), 0)
    ss = lax.broadcasted_iota(jnp.int32, (K_BLOCK, K_BLOCK), 1)
    tri = (jj > ss).astype(BF16)
    return pl.pallas_call(
        _attn_kernel,
        grid=(batch, n_grp, n_q),
        in_specs=[
            pl.BlockSpec((Q_BLOCK, width), lambda b, j, i: (b * n_q + i, j)),
            pl.BlockSpec((seq, width), lambda b, j, i: (b, n_grp + j)),
            pl.BlockSpec((seq, width), lambda b, j, i: (b, 2 * n_grp + j)),
            _resident((K_BLOCK, K_BLOCK)),
        ],
        out_specs=pl.BlockSpec((Q_BLOCK, width), lambda b, j, i: (b * n_q + i, j)),
        out_shape=jax.ShapeDtypeStruct((t, D_SB), BF16),
        scratch_shapes=[
            pltpu.VMEM((PAIRS_PER_STEP, 2 * Q_BLOCK, LANES), F32),
            pltpu.VMEM((PAIRS_PER_STEP, 2 * Q_BLOCK, 1), F32),
        ],
        compiler_params=pltpu.CompilerParams(
            dimension_semantics=("arbitrary", "arbitrary", "arbitrary"),
            vmem_limit_bytes=VMEM_LIMIT),
        name="attn",
    )(qkv, qkv, qkv, tri)


def _post_kernel(seq, u_ref, halo_ref, yb_ref, g_ref, x_ref, p_ref,
                 wpool_ref, pscale_ref, wa_ref, wb_ref, wout_ref,
                 gffn_ref, wgate_ref, wup_ref, wdown_ref,
                 gple_ref, wpg_ref, wpp_ref, gfin_ref, o_ref, ubuf_ref):
    i = pl.program_id(0)
    t0 = (i * POST_TILE) % seq
    u = u_ref[...]
    halo = halo_ref[...]
    ubuf_ref[0:POOL_HALO, :] = jnp.where(t0 == 0, jnp.zeros_like(halo), halo)
    ubuf_ref[POOL_HALO:, :] = u
    pos = t0 + lax.broadcasted_iota(jnp.int32, (POST_TILE, 1), 0)
    mixed = []
    for g, w in enumerate(POOL_WINDOWS):
        cols = slice(g * POOL_GROUP_DIM, (g + 1) * POOL_GROUP_DIM)
        win = u[:, cols]
        for j in range(1, w):
            win = win + ubuf_ref[POOL_HALO - j:POOL_HALO - j + POST_TILE, cols]
        count = jnp.minimum(pos + 1, w).astype(F32)
        pooled = win / count - u[:, cols]
        mixed.append(_dot(pooled.astype(BF16), wpool_ref[g]))
    y_a = jnp.concatenate(mixed, axis=1) * pscale_ref[...]

    g_all = g_ref[...]
    merged = (jax.nn.sigmoid(g_all[:, :D_MODEL]) * _dot(y_a.astype(BF16), wa_ref[...])
              + jax.nn.sigmoid(g_all[:, D_MODEL:]) * _dot(yb_ref[...], wb_ref[...]))
    x = x_ref[...] + _dot(merged.astype(BF16), wout_ref[...])

    h = _rms(x, gffn_ref[...]).astype(BF16)
    act = jax.nn.silu(_dot(h, wgate_ref[...])) * _dot(h, wup_ref[...])
    x = x + _dot(act.astype(BF16), wdown_ref[...])

    h = _rms(x, gple_ref[...]).astype(BF16)
    gate = jax.nn.sigmoid(_dot(h, wpg_ref[...]))
    x = x + gate * _dot(p_ref[...].astype(BF16), wpp_ref[...])
    o_ref[...] = _rms(x, gfin_ref[...])


def _post(u, y_b, g, x2d, p2d, seq, w_pool, pool_scale, w_a, w_b, w_out,
          gain_ffn, w_gate, w_up, w_down, gain_ple, w_pg, w_pp, gain_final):
    t = x2d.shape[0]
    halo_per_tile = POST_TILE // POOL_HALO
    weights = (w_pool, pool_scale, w_a, w_b, w_out, gain_ffn, w_gate, w_up, w_down,
               gain_ple, w_pg, w_pp, gain_final)
    return pl.pallas_call(
        functools.partial(_post_kernel, seq),
        grid=(t // POST_TILE,),
        in_specs=[
            _rows(POST_TILE, D_POOL),
            pl.BlockSpec((POOL_HALO, D_POOL), lambda i: (jnp.maximum(i * halo_per_tile - 1, 0), 0)),
            _rows(POST_TILE, D_SB),
            _rows(POST_TILE, 2 * D_MODEL),
            _rows(POST_TILE, D_MODEL),
            _rows(POST_TILE, PLE_DIM),
        ] + [_resident(w.shape) for w in weights],
        out_specs=_rows(POST_TILE, D_MODEL),
        out_shape=jax.ShapeDtypeStruct((t, D_MODEL), F32),
        scratch_shapes=[pltpu.VMEM((POST_TILE + POOL_HALO, D_POOL), F32)],
        compiler_params=pltpu.CompilerParams(
            dimension_semantics=("arbitrary",), vmem_limit_bytes=VMEM_LIMIT),
        name="post",
    )(u, u, y_b, g, x2d, p2d, *weights)


def kernel(x, p, norm_mix, w_in, w_pool, pool_scale, w_branch_a, w_branch_b, w_out, norm_ffn,
           w_ffn_gate, w_ffn_up, w_ffn_down, norm_ple, w_ple_gate, w_ple_proj, norm_final):
    batch, seq, d_model = x.shape
    assert w_in.shape[0] == 1 and d_model == D_MODEL
    assert seq % PROJ_TILE == 0 and seq % POST_TILE == 0 and seq % Q_BLOCK == 0
    t = batch * seq
    x2d = x.reshape(t, D_MODEL)
    row = lambda v: v.reshape(1, -1).astype(F32)
    bf = lambda w: w[0].astype(BF16)
    u, qkv, g = _proj(x2d, row(norm_mix[0]), bf(w_in))
    y_b = _attention(qkv, batch, seq)
    out = _post(u, y_b, g, x2d, p[0].reshape(t, PLE_DIM), seq,
                bf(w_pool), row(pool_scale[0]), bf(w_branch_a), bf(w_branch_b), bf(w_out),
                row(norm_ffn[0]), bf(w_ffn_gate), bf(w_ffn_up), bf(w_ffn_down),
                row(norm_ple[0]), bf(w_ple_gate), bf(w_ple_proj), row(norm_final))
    return out.reshape(batch, seq, D_MODEL)
```

```python
import functools

import jax
import jax.numpy as jnp
from jax import lax
from jax.experimental import pallas as pl
from jax.experimental.pallas import tpu as pltpu

D_MODEL = 1024
PLE_DIM = 256
D_POOL = D_MODEL // 2
POOL_WINDOWS = (2, 4, 8, 16)
POOL_GROUP_DIM = D_POOL // len(POOL_WINDOWS)
POOL_HALO = 16
SB_HEADS = 8
SB_HEAD_DIM = 64
D_SB = SB_HEADS * SB_HEAD_DIM
D_FF = ((8 * D_MODEL // 3 + 255) // 256) * 256
D_IN = D_POOL + 3 * D_SB + 2 * D_MODEL
RMS_EPS = 1e-6

LANES = 128
N_PAIRS = D_SB // LANES
POST_TILE = 512
Q_BLOCK = 256
K_BLOCK = Q_BLOCK
PIECE = 256
VMEM_LIMIT = 56 * 1024 * 1024
EXIT_LOG = -104.0

LOG2E = 1.4426950408889634

F32 = jnp.float32
BF16 = jnp.bfloat16


def _dot(a, b):
    return jnp.dot(a, b, preferred_element_type=F32)


def _rms(xf, gain_row):
    inv = lax.rsqrt(jnp.mean(xf * xf, axis=-1, keepdims=True) + RMS_EPS)
    return xf * inv * gain_row


def _resident(shape):
    zeros = (0,) * len(shape)
    return pl.BlockSpec(shape, lambda *_: zeros, pipeline_mode=pl.Buffered(1))


def _rows(tile, width):
    return pl.BlockSpec((tile, width), lambda i: (i, 0))


def _sweep(qs, kblk, vblk, tri, carry, diagonal, filler=None):
    rows = qs.shape[0]
    nz = lax.dot_general(qs, kblk, (((1,), (1,)), ((), ())),
                         preferred_element_type=F32)
    lf = jnp.minimum(nz, 0.0) - jnp.log(1.0 + jnp.exp2(jnp.abs(nz) * (-LOG2E)))
    if diagonal:
        t_loc = lax.broadcasted_iota(jnp.int32, (rows, K_BLOCK), 0) & (Q_BLOCK - 1)
        s_loc = lax.broadcasted_iota(jnp.int32, (rows, K_BLOCK), 1)
        mask = s_loc < t_loc
        lf = jnp.where(mask, lf, 0.0)
    if filler is not None:
        filler()
    suffix = _dot(lf.astype(BF16), tri)
    a = jnp.exp((lf - nz) + suffix)
    if diagonal:
        a = jnp.where(mask, a, 0.0)
    if filler is not None:
        filler()
    pv = _dot(a.astype(BF16), vblk)
    if carry is not None:
        pv = pv * jnp.exp(carry)
    return pv, jnp.sum(lf, axis=1, keepdims=True)


def _mix_kernel(seq, x_ref, gain_ref, w_ref, tri_ref, u_ref, g_ref, yb_ref,
                kc_ref, vc_ref, acc_ref, carry_ref):
    qi = lax.rem(pl.program_id(0), seq // Q_BLOCK)
    row0 = pl.multiple_of(qi * Q_BLOCK, Q_BLOCK)
    h = _rms(x_ref[...], gain_ref[...]).astype(BF16)
    q = (_dot(h, w_ref[:, D_POOL:D_POOL + D_SB]) * (-1.0 / 8.0)).astype(BF16)
    k = _dot(h, w_ref[:, D_POOL + D_SB:D_POOL + 2 * D_SB]).astype(BF16)
    v = _dot(h, w_ref[:, D_POOL + 2 * D_SB:D_POOL + 3 * D_SB]).astype(BF16)
    kc_ref[pl.ds(row0, Q_BLOCK), :] = k
    vc_ref[pl.ds(row0, Q_BLOCK), :] = v

    lane = lax.broadcasted_iota(jnp.int32, (Q_BLOCK, LANES), 1)
    tri = tri_ref[...]
    pairs = [slice(pr * LANES, (pr + 1) * LANES) for pr in range(N_PAIRS)]

    def stacked_q(cols):
        q2 = q[:, cols]
        zero = jnp.zeros_like(q2)
        return jnp.concatenate([jnp.where(lane < SB_HEAD_DIM, q2, zero),
                                jnp.where(lane >= SB_HEAD_DIM, q2, zero)], axis=0)

    def cached(ref, kb, cols):
        return ref[pl.ds(pl.multiple_of(kb * K_BLOCK, K_BLOCK), K_BLOCK), cols]

    has_prev = qi > 0
    prev = jnp.maximum(qi - 1, 0)

    pieces = [(u_ref, c, c) for c in range(0, D_POOL, PIECE)]
    pieces += [(g_ref, c, D_POOL + 3 * D_SB + c) for c in range(0, 2 * D_MODEL, PIECE)]

    def project_pieces(n):
        for _ in range(min(n, len(pieces))):
            ref, c, wc = pieces.pop(0)
            ref[:, c:c + PIECE] = _dot(h, w_ref[:, wc:wc + PIECE])

    for pr, cols in enumerate(pairs):
        qs = stacked_q(cols)
        fill = functools.partial(project_pieces, 1)
        pv_d, tot_d = _sweep(qs, k[:, cols], v[:, cols], tri, None, True, fill)
        pv_p, tot_p = _sweep(qs, cached(kc_ref, prev, cols), cached(vc_ref, prev, cols), tri,
                             tot_d, False, fill)
        acc_ref[pr] = pv_d + jnp.where(has_prev, pv_p, 0.0)
        carry_ref[pr] = tot_d + jnp.where(has_prev, tot_p, 0.0)
    project_pieces(len(pieces))

    def live():
        return (jnp.max(carry_ref[...]) >= EXIT_LOG).astype(jnp.int32)

    def keep_going(state):
        kb, flag = state
        return jnp.logical_and(kb >= 0, flag > 0)

    def body(state):
        kb, _ = state
        for pr, cols in enumerate(pairs):
            carry = carry_ref[pr]
            pv, tot = _sweep(stacked_q(cols), cached(kc_ref, kb, cols), cached(vc_ref, kb, cols),
                             tri, carry, False)
            acc_ref[pr] += pv
            carry_ref[pr] = carry + tot
        return kb - 1, live()

    lax.while_loop(keep_going, body, (qi - 2, live()))

    for pr, cols in enumerate(pairs):
        acc = acc_ref[pr]
        yb_ref[:, cols] = jnp.where(lane < SB_HEAD_DIM, acc[:Q_BLOCK], acc[Q_BLOCK:]).astype(BF16)


def _mix(x2d, gain, w_in, seq):
    t = x2d.shape[0]
    jj = lax.broadcasted_iota(jnp.int32, (K_BLOCK, K_BLOCK), 0)
    ss = lax.broadcasted_iota(jnp.int32, (K_BLOCK, K_BLOCK), 1)
    tri = (jj > ss).astype(BF16)
    return pl.pallas_call(
        functools.partial(_mix_kernel, seq),
        grid=(t // Q_BLOCK,),
        in_specs=[
            _rows(Q_BLOCK, D_MODEL),
            _resident((1, D_MODEL)),
            _resident((D_MODEL, D_IN)),
            _resident((K_BLOCK, K_BLOCK)),
        ],
        out_specs=[
            _rows(Q_BLOCK, D_POOL),
            _rows(Q_BLOCK, 2 * D_MODEL),
            _rows(Q_BLOCK, D_SB),
        ],
        out_shape=[
            jax.ShapeDtypeStruct((t, D_POOL), F32),
            jax.ShapeDtypeStruct((t, 2 * D_MODEL), F32),
            jax.ShapeDtypeStruct((t, D_SB), BF16),
        ],
        scratch_shapes=[
            pltpu.VMEM((seq, D_SB), BF16),
            pltpu.VMEM((seq, D_SB), BF16),
            pltpu.VMEM((N_PAIRS, 2 * Q_BLOCK, LANES), F32),
            pltpu.VMEM((N_PAIRS, 2 * Q_BLOCK, 1), F32),
        ],
        compiler_params=pltpu.CompilerParams(
            dimension_semantics=("arbitrary",), vmem_limit_bytes=VMEM_LIMIT),
        name="mix",
    )(x2d, gain, w_in, tri)


def _post_kernel(seq, u_ref, halo_ref, yb_ref, g_ref, x_ref, p_ref,
                 wpool_ref, pscale_ref, wa_ref, wb_ref, wout_ref,
                 gffn_ref, wgate_ref, wup_ref, wdown_ref,
                 gple_ref, wpg_ref, wpp_ref, gfin_ref, o_ref, ubuf_ref):
    i = pl.program_id(0)
    t0 = (i * POST_TILE) % seq
    u = u_ref[...]
    halo = halo_ref[...]
    ubuf_ref[0:POOL_HALO, :] = jnp.where(t0 == 0, jnp.zeros_like(halo), halo)
    ubuf_ref[POOL_HALO:, :] = u
    pos = t0 + lax.broadcasted_iota(jnp.int32, (POST_TILE, 1), 0)
    mixed = []
    for g, w in enumerate(POOL_WINDOWS):
        cols = slice(g * POOL_GROUP_DIM, (g + 1) * POOL_GROUP_DIM)
        win = u[:, cols]
        for j in range(1, w):
            win = win + ubuf_ref[POOL_HALO - j:POOL_HALO - j + POST_TILE, cols]
        count = jnp.minimum(pos + 1, w).astype(F32)
        pooled = win / count - u[:, cols]
        mixed.append(_dot(pooled.astype(BF16), wpool_ref[g]))
    y_a = jnp.concatenate(mixed, axis=1) * pscale_ref[...]

    g_all = g_ref[...]
    merged = (jax.nn.sigmoid(g_all[:, :D_MODEL]) * _dot(y_a.astype(BF16), wa_ref[...])
              + jax.nn.sigmoid(g_all[:, D_MODEL:]) * _dot(yb_ref[...], wb_ref[...]))
    x = x_ref[...] + _dot(merged.astype(BF16), wout_ref[...])

    h = _rms(x, gffn_ref[...]).astype(BF16)
    act = jax.nn.silu(_dot(h, wgate_ref[...])) * _dot(h, wup_ref[...])
    x = x + _dot(act.astype(BF16), wdown_ref[...])

    h = _rms(x, gple_ref[...]).astype(BF16)
    gate = jax.nn.sigmoid(_dot(h, wpg_ref[...]))
    x = x + gate * _dot(p_ref[...].astype(BF16), wpp_ref[...])
    o_ref[...] = _rms(x, gfin_ref[...])


def _post(u, y_b, g, x2d, p2d, seq, w_pool, pool_scale, w_a, w_b, w_out,
          gain_ffn, w_gate, w_up, w_down, gain_ple, w_pg, w_pp, gain_final):
    t = x2d.shape[0]
    halo_per_tile = POST_TILE // POOL_HALO
    weights = (w_pool, pool_scale, w_a, w_b, w_out, gain_ffn, w_gate, w_up, w_down,
               gain_ple, w_pg, w_pp, gain_final)
    return pl.pallas_call(
        functools.partial(_post_kernel, seq),
        grid=(t // POST_TILE,),
        in_specs=[
            _rows(POST_TILE, D_POOL),
            pl.BlockSpec((POOL_HALO, D_POOL), lambda i: (jnp.maximum(i * halo_per_tile - 1, 0), 0)),
            _rows(POST_TILE, D_SB),
            _rows(POST_TILE, 2 * D_MODEL),
            _rows(POST_TILE, D_MODEL),
            _rows(POST_TILE, PLE_DIM),
        ] + [_resident(w.shape) for w in weights],
        out_specs=_rows(POST_TILE, D_MODEL),
        out_shape=jax.ShapeDtypeStruct((t, D_MODEL), F32),
        scratch_shapes=[pltpu.VMEM((POST_TILE + POOL_HALO, D_POOL), F32)],
        compiler_params=pltpu.CompilerParams(
            dimension_semantics=("arbitrary",), vmem_limit_bytes=VMEM_LIMIT),
        name="post",
    )(u, u, y_b, g, x2d, p2d, *weights)


def kernel(x, p, norm_mix, w_in, w_pool, pool_scale, w_branch_a, w_branch_b, w_out, norm_ffn,
           w_ffn_gate, w_ffn_up, w_ffn_down, norm_ple, w_ple_gate, w_ple_proj, norm_final):
    batch, seq, d_model = x.shape
    assert w_in.shape[0] == 1 and d_model == D_MODEL
    assert seq % POST_TILE == 0 and seq % Q_BLOCK == 0
    t = batch * seq
    x2d = x.reshape(t, D_MODEL)
    row = lambda v: v.reshape(1, -1).astype(F32)
    bf = lambda w: w[0].astype(BF16)
    u, g, y_b = _mix(x2d, row(norm_mix[0]), bf(w_in), seq)
    out = _post(u, y_b, g, x2d, p[0].reshape(t, PLE_DIM), seq,
                bf(w_pool), row(pool_scale[0]), bf(w_branch_a), bf(w_branch_b), bf(w_out),
                row(norm_ffn[0]), bf(w_ffn_gate), bf(w_ffn_up), bf(w_ffn_down),
                row(norm_ple[0]), bf(w_ple_gate), bf(w_ple_proj), row(norm_final))
    return out.reshape(batch, seq, D_MODEL)
```

```python
import functools

import jax
import jax.numpy as jnp
from jax import lax
from jax.experimental import pallas as pl
from jax.experimental.pallas import tpu as pltpu

D_MODEL = 1024
PLE_DIM = 256
D_POOL = D_MODEL // 2
POOL_WINDOWS = (2, 4, 8, 16)
POOL_GROUP_DIM = D_POOL // len(POOL_WINDOWS)
POOL_HALO = 16
SB_HEADS = 8
SB_HEAD_DIM = 64
D_SB = SB_HEADS * SB_HEAD_DIM
D_FF = ((8 * D_MODEL // 3 + 255) // 256) * 256
D_IN = D_POOL + 3 * D_SB + 2 * D_MODEL
RMS_EPS = 1e-6

LANES = 128
N_PAIRS = D_SB // LANES
POST_TILE = 512
Q_BLOCK = 256
K_BLOCK = Q_BLOCK
PIECE = 256
VMEM_LIMIT = 56 * 1024 * 1024
EXIT_LOG = -104.0

LOG2E = 1.4426950408889634

F32 = jnp.float32
BF16 = jnp.bfloat16


def _dot(a, b):
    return jnp.dot(a, b, preferred_element_type=F32)


def _rms(xf, gain_row):
    inv = lax.rsqrt(jnp.mean(xf * xf, axis=-1, keepdims=True) + RMS_EPS)
    return xf * inv * gain_row


def _resident(shape):
    zeros = (0,) * len(shape)
    return pl.BlockSpec(shape, lambda *_: zeros, pipeline_mode=pl.Buffered(1))


def _rows(tile, width):
    return pl.BlockSpec((tile, width), lambda i: (i, 0))


def _sweep(qs, kblk, vblk, tri, carry, diagonal, filler=None):
    rows = qs.shape[0]
    nz = lax.dot_general(qs, kblk, (((1,), (1,)), ((), ())),
                         preferred_element_type=F32)
    lf = jnp.minimum(nz, 0.0) - jnp.log(1.0 + jnp.exp2(jnp.abs(nz) * (-LOG2E)))
    if diagonal:
        t_loc = lax.broadcasted_iota(jnp.int32, (rows, K_BLOCK), 0) & (Q_BLOCK - 1)
        s_loc = lax.broadcasted_iota(jnp.int32, (rows, K_BLOCK), 1)
        mask = s_loc < t_loc
        lf = jnp.where(mask, lf, 0.0)
    if filler is not None:
        filler()
    suffix = _dot(lf.astype(BF16), tri)
    a = jnp.exp((lf - nz) + suffix)
    if diagonal:
        a = jnp.where(mask, a, 0.0)
    if filler is not None:
        filler()
    pv = _dot(a.astype(BF16), vblk)
    if carry is not None:
        pv = pv * jnp.exp(carry)
    return pv, jnp.sum(lf, axis=1, keepdims=True)


def _mix_kernel(seq, x_ref, gain_ref, w_ref, tri_ref, u_ref, g_ref, yb_ref,
                kc_ref, vc_ref, qst_ref, kst_ref, vst_ref, acc_ref, carry_ref):
    i = pl.program_id(0)
    n_blk = seq // Q_BLOCK

    @pl.when(i == 0)
    def _():
        kc_ref[...] = jnp.zeros_like(kc_ref)
        vc_ref[...] = jnp.zeros_like(vc_ref)
        qst_ref[...] = jnp.zeros_like(qst_ref)

    qi = lax.rem(i + (n_blk - 1), n_blk)
    q = qst_ref[lax.rem(i + 1, 2)]
    q_slot = lax.rem(i, 2)
    h = _rms(x_ref[...], gain_ref[...]).astype(BF16)

    def store_piece(c):
        acc = _dot(h, w_ref[:, c:c + PIECE])
        if c < D_POOL:
            u_ref[:, c:c + PIECE] = acc
        elif c < D_POOL + D_SB:
            qst_ref[q_slot, :, c - D_POOL:c - D_POOL + PIECE] = (acc * (-1.0 / 8.0)).astype(BF16)
        elif c < D_POOL + 2 * D_SB:
            kst_ref[:, c - D_POOL - D_SB:c - D_POOL - D_SB + PIECE] = acc.astype(BF16)
        elif c < D_POOL + 3 * D_SB:
            vst_ref[:, c - D_POOL - 2 * D_SB:c - D_POOL - 2 * D_SB + PIECE] = acc.astype(BF16)
        else:
            g_ref[:, c - D_POOL - 3 * D_SB:c - D_POOL - 3 * D_SB + PIECE] = acc

    pieces = list(range(0, D_IN, PIECE))

    def fill():
        if pieces:
            store_piece(pieces.pop(0))

    lane = lax.broadcasted_iota(jnp.int32, (Q_BLOCK, LANES), 1)
    tri = tri_ref[...]
    pairs = [slice(pr * LANES, (pr + 1) * LANES) for pr in range(N_PAIRS)]

    def stacked_q(cols):
        q2 = q[:, cols]
        zero = jnp.zeros_like(q2)
        return jnp.concatenate([jnp.where(lane < SB_HEAD_DIM, q2, zero),
                                jnp.where(lane >= SB_HEAD_DIM, q2, zero)], axis=0)

    def cached(ref, kb, cols):
        return ref[pl.ds(pl.multiple_of(kb * K_BLOCK, K_BLOCK), K_BLOCK), cols]

    has_prev = qi > 0
    prev = jnp.maximum(qi - 1, 0)
    for pr, cols in enumerate(pairs):
        qs = stacked_q(cols)
        pv_d, tot_d = _sweep(qs, cached(kc_ref, qi, cols), cached(vc_ref, qi, cols), tri,
                             None, True, fill)
        pv_p, tot_p = _sweep(qs, cached(kc_ref, prev, cols), cached(vc_ref, prev, cols), tri,
                             tot_d, False, fill)
        acc_ref[pr] = pv_d + jnp.where(has_prev, pv_p, 0.0)
        carry_ref[pr] = tot_d + jnp.where(has_prev, tot_p, 0.0)
    while pieces:
        fill()

    def live():
        return (jnp.max(carry_ref[...]) >= EXIT_LOG).astype(jnp.int32)

    def keep_going(state):
        kb, flag = state
        return jnp.logical_and(kb >= 0, flag > 0)

    def body(state):
        kb, _ = state
        for pr, cols in enumerate(pairs):
            carry = carry_ref[pr]
            pv, tot = _sweep(stacked_q(cols), cached(kc_ref, kb, cols), cached(vc_ref, kb, cols),
                             tri, carry, False)
            acc_ref[pr] += pv
            carry_ref[pr] = carry + tot
        return kb - 1, live()

    lax.while_loop(keep_going, body, (qi - 2, live()))

    for pr, cols in enumerate(pairs):
        acc = acc_ref[pr]
        yb_ref[:, cols] = jnp.where(lane < SB_HEAD_DIM, acc[:Q_BLOCK], acc[Q_BLOCK:]).astype(BF16)

    row0 = pl.multiple_of(lax.rem(i, n_blk) * Q_BLOCK, Q_BLOCK)
    kc_ref[pl.ds(row0, Q_BLOCK), :] = kst_ref[...]
    vc_ref[pl.ds(row0, Q_BLOCK), :] = vst_ref[...]


def _mix(x2d, gain, w_in, seq):
    t = x2d.shape[0]
    n_tiles = t // Q_BLOCK
    jj = lax.broadcasted_iota(jnp.int32, (K_BLOCK, K_BLOCK), 0)
    ss = lax.broadcasted_iota(jnp.int32, (K_BLOCK, K_BLOCK), 1)
    tri = (jj > ss).astype(BF16)
    this_tile = lambda i: (jnp.minimum(i, n_tiles - 1), 0)
    last_tile = lambda i: (jnp.maximum(i - 1, 0), 0)
    return pl.pallas_call(
        functools.partial(_mix_kernel, seq),
        grid=(n_tiles + 1,),
        in_specs=[
            pl.BlockSpec((Q_BLOCK, D_MODEL), this_tile),
            _resident((1, D_MODEL)),
            _resident((D_MODEL, D_IN)),
            _resident((K_BLOCK, K_BLOCK)),
        ],
        out_specs=[
            pl.BlockSpec((Q_BLOCK, D_POOL), this_tile),
            pl.BlockSpec((Q_BLOCK, 2 * D_MODEL), this_tile),
            pl.BlockSpec((Q_BLOCK, D_SB), last_tile),
        ],
        out_shape=[
            jax.ShapeDtypeStruct((t, D_POOL), F32),
            jax.ShapeDtypeStruct((t, 2 * D_MODEL), F32),
            jax.ShapeDtypeStruct((t, D_SB), BF16),
        ],
        scratch_shapes=[
            pltpu.VMEM((seq, D_SB), BF16),
            pltpu.VMEM((seq, D_SB), BF16),
            pltpu.VMEM((2, Q_BLOCK, D_SB), BF16),
            pltpu.VMEM((Q_BLOCK, D_SB), BF16),
            pltpu.VMEM((Q_BLOCK, D_SB), BF16),
            pltpu.VMEM((N_PAIRS, 2 * Q_BLOCK, LANES), F32),
            pltpu.VMEM((N_PAIRS, 2 * Q_BLOCK, 1), F32),
        ],
        compiler_params=pltpu.CompilerParams(
            dimension_semantics=("arbitrary",), vmem_limit_bytes=VMEM_LIMIT),
        name="mix",
    )(x2d, gain, w_in, tri)


def _post_kernel(seq, u_ref, halo_ref, yb_ref, g_ref, x_ref, p_ref,
                 wpool_ref, pscale_ref, wa_ref, wb_ref, wout_ref,
                 gffn_ref, wgate_ref, wup_ref, wdown_ref,
                 gple_ref, wpg_ref, wpp_ref, gfin_ref, o_ref, ubuf_ref):
    i = pl.program_id(0)
    t0 = (i * POST_TILE) % seq
    u = u_ref[...]
    halo = halo_ref[...]
    ubuf_ref[0:POOL_HALO, :] = jnp.where(t0 == 0, jnp.zeros_like(halo), halo)
    ubuf_ref[POOL_HALO:, :] = u
    pos = t0 + lax.broadcasted_iota(jnp.int32, (POST_TILE, 1), 0)
    mixed = []
    for g, w in enumerate(POOL_WINDOWS):
        cols = slice(g * POOL_GROUP_DIM, (g + 1) * POOL_GROUP_DIM)
        win = u[:, cols]
        for j in range(1, w):
            win = win + ubuf_ref[POOL_HALO - j:POOL_HALO - j + POST_TILE, cols]
        count = jnp.minimum(pos + 1, w).astype(F32)
        pooled = win / count - u[:, cols]
        mixed.append(_dot(pooled.astype(BF16), wpool_ref[g]))
    y_a = jnp.concatenate(mixed, axis=1) * pscale_ref[...]

    g_all = g_ref[...]
    merged = (jax.nn.sigmoid(g_all[:, :D_MODEL]) * _dot(y_a.astype(BF16), wa_ref[...])
              + jax.nn.sigmoid(g_all[:, D_MODEL:]) * _dot(yb_ref[...], wb_ref[...]))
    x = x_ref[...] + _dot(merged.astype(BF16), wout_ref[...])

    h = _rms(x, gffn_ref[...]).astype(BF16)
    act = jax.nn.silu(_dot(h, wgate_ref[...])) * _dot(h, wup_ref[...])
    x = x + _dot(act.astype(BF16), wdown_ref[...])

    h = _rms(x, gple_ref[...]).astype(BF16)
    gate = jax.nn.sigmoid(_dot(h, wpg_ref[...]))
    x = x + gate * _dot(p_ref[...].astype(BF16), wpp_ref[...])
    o_ref[...] = _rms(x, gfin_ref[...])


def _post(u, y_b, g, x2d, p2d, seq, w_pool, pool_scale, w_a, w_b, w_out,
          gain_ffn, w_gate, w_up, w_down, gain_ple, w_pg, w_pp, gain_final):
    t = x2d.shape[0]
    halo_per_tile = POST_TILE // POOL_HALO
    weights = (w_pool, pool_scale, w_a, w_b, w_out, gain_ffn, w_gate, w_up, w_down,
               gain_ple, w_pg, w_pp, gain_final)
    return pl.pallas_call(
        functools.partial(_post_kernel, seq),
        grid=(t // POST_TILE,),
        in_specs=[
            _rows(POST_TILE, D_POOL),
            pl.BlockSpec((POOL_HALO, D_POOL), lambda i: (jnp.maximum(i * halo_per_tile - 1, 0), 0)),
            _rows(POST_TILE, D_SB),
            _rows(POST_TILE, 2 * D_MODEL),
            _rows(POST_TILE, D_MODEL),
            _rows(POST_TILE, PLE_DIM),
        ] + [_resident(w.shape) for w in weights],
        out_specs=_rows(POST_TILE, D_MODEL),
        out_shape=jax.ShapeDtypeStruct((t, D_MODEL), F32),
        scratch_shapes=[pltpu.VMEM((POST_TILE + POOL_HALO, D_POOL), F32)],
        compiler_params=pltpu.CompilerParams(
            dimension_semantics=("arbitrary",), vmem_limit_bytes=VMEM_LIMIT),
        name="post",
    )(u, u, y_b, g, x2d, p2d, *weights)


def kernel(x, p, norm_mix, w_in, w_pool, pool_scale, w_branch_a, w_branch_b, w_out, norm_ffn,
           w_ffn_gate, w_ffn_up, w_ffn_down, norm_ple, w_ple_gate, w_ple_proj, norm_final):
    batch, seq, d_model = x.shape
    assert w_in.shape[0] == 1 and d_model == D_MODEL
    assert seq % POST_TILE == 0 and seq % Q_BLOCK == 0
    t = batch * seq
    x2d = x.reshape(t, D_MODEL)
    row = lambda v: v.reshape(1, -1).astype(F32)
    bf = lambda w: w[0].astype(BF16)
    u, g, y_b = _mix(x2d, row(norm_mix[0]), bf(w_in), seq)
    out = _post(u, y_b, g, x2d, p[0].reshape(t, PLE_DIM), seq,
                bf(w_pool), row(pool_scale[0]), bf(w_branch_a), bf(w_branch_b), bf(w_out),
                row(norm_ffn[0]), bf(w_ffn_gate), bf(w_ffn_up), bf(w_ffn_down),
                row(norm_ple[0]), bf(w_ple_gate), bf(w_ple_proj), row(norm_final))
    return out.reshape(batch, seq, D_MODEL)
```

```python
import functools

import jax
import jax.numpy as jnp
from jax import lax
from jax.experimental import pallas as pl
from jax.experimental.pallas import tpu as pltpu

D_MODEL = 1024
PLE_DIM = 256
D_POOL = D_MODEL // 2
POOL_WINDOWS = (2, 4, 8, 16)
POOL_GROUP_DIM = D_POOL // len(POOL_WINDOWS)
POOL_HALO = 16
SB_HEADS = 8
SB_HEAD_DIM = 64
D_SB = SB_HEADS * SB_HEAD_DIM
D_FF = ((8 * D_MODEL // 3 + 255) // 256) * 256
D_IN = D_POOL + 3 * D_SB + 2 * D_MODEL
RMS_EPS = 1e-6

LANES = 128
SUBLANES = 8
N_PAIRS = D_SB // LANES
POST_TILE = 512
Q_BLOCK = 256
K_BLOCK = Q_BLOCK
PIECE = 256
VMEM_LIMIT = 56 * 1024 * 1024
EXIT_LOG = -104.0

LOG2E = 1.4426950408889634

F32 = jnp.float32
BF16 = jnp.bfloat16


def _dot(a, b):
    return jnp.dot(a, b, preferred_element_type=F32)


def _rms(xf, gain_row):
    inv = lax.rsqrt(jnp.mean(xf * xf, axis=-1, keepdims=True) + RMS_EPS)
    return xf * inv * gain_row


def _resident(shape):
    zeros = (0,) * len(shape)
    return pl.BlockSpec(shape, lambda *_: zeros, pipeline_mode=pl.Buffered(1))


def _rows(tile, width):
    return pl.BlockSpec((tile, width), lambda i: (i, 0))


def _sweep(qs, kblk, vblk, tri, carry, diagonal, filler=None):
    rows = qs.shape[0]
    nz = lax.dot_general(qs, kblk, (((1,), (1,)), ((), ())),
                         preferred_element_type=F32)
    lf = jnp.minimum(nz, 0.0) - jnp.log(1.0 + jnp.exp2(jnp.abs(nz) * (-LOG2E)))
    if diagonal:
        t_loc = lax.broadcasted_iota(jnp.int32, (rows, K_BLOCK), 0) & (Q_BLOCK - 1)
        s_loc = lax.broadcasted_iota(jnp.int32, (rows, K_BLOCK), 1)
        mask = s_loc < t_loc
        lf = jnp.where(mask, lf, 0.0)
    if filler is not None:
        filler()
    suffix = _dot(lf.astype(BF16), tri)
    a = jnp.exp((lf - nz) + suffix)
    if diagonal:
        a = jnp.where(mask, a, 0.0)
    if filler is not None:
        filler()
    pv = _dot(a.astype(BF16), vblk)
    if carry is not None:
        pv = pv * jnp.exp(carry)
    return pv, jnp.sum(lf, axis=1, keepdims=True)


def _mix_kernel(seq, x_ref, gain_ref, w_ref, tri_ref, u_ref, g_ref, yb_ref,
                kc_ref, vc_ref, acc_ref, carry_ref):
    qi = lax.rem(pl.program_id(0), seq // Q_BLOCK)
    row0 = pl.multiple_of(qi * Q_BLOCK, Q_BLOCK)
    h = _rms(x_ref[...], gain_ref[...]).astype(BF16)
    q = (_dot(h, w_ref[:, D_POOL:D_POOL + D_SB]) * (-1.0 / 8.0)).astype(BF16)
    k = _dot(h, w_ref[:, D_POOL + D_SB:D_POOL + 2 * D_SB]).astype(BF16)
    v = _dot(h, w_ref[:, D_POOL + 2 * D_SB:D_POOL + 3 * D_SB]).astype(BF16)
    kc_ref[pl.ds(row0, Q_BLOCK), :] = k
    vc_ref[pl.ds(row0, Q_BLOCK), :] = v

    lane = lax.broadcasted_iota(jnp.int32, (Q_BLOCK, LANES), 1)
    tri = tri_ref[...]
    pairs = [slice(pr * LANES, (pr + 1) * LANES) for pr in range(N_PAIRS)]

    def stacked_q(cols):
        q2 = q[:, cols]
        zero = jnp.zeros_like(q2)
        return jnp.concatenate([jnp.where(lane < SB_HEAD_DIM, q2, zero),
                                jnp.where(lane >= SB_HEAD_DIM, q2, zero)], axis=0)

    def cached(ref, kb, cols):
        return ref[pl.ds(pl.multiple_of(kb * K_BLOCK, K_BLOCK), K_BLOCK), cols]

    has_prev = qi > 0
    prev = jnp.maximum(qi - 1, 0)

    pieces = [(u_ref, c, c) for c in range(0, D_POOL, PIECE)]
    pieces += [(g_ref, c, D_POOL + 3 * D_SB + c) for c in range(0, 2 * D_MODEL, PIECE)]

    def project_pieces(n):
        for _ in range(min(n, len(pieces))):
            ref, c, wc = pieces.pop(0)
            ref[:, c:c + PIECE] = _dot(h, w_ref[:, wc:wc + PIECE])

    for pr, cols in enumerate(pairs):
        qs = stacked_q(cols)
        fill = functools.partial(project_pieces, 1)
        pv_d, tot_d = _sweep(qs, k[:, cols], v[:, cols], tri, None, True, fill)
        pv_p, tot_p = _sweep(qs, cached(kc_ref, prev, cols), cached(vc_ref, prev, cols), tri,
                             tot_d, False, fill)
        acc_ref[pr] = pv_d + jnp.where(has_prev, pv_p, 0.0)
        carry_ref[pr] = tot_d + jnp.where(has_prev, tot_p, 0.0)
    project_pieces(len(pieces))

    def live():
        return (jnp.max(carry_ref[...]) >= EXIT_LOG).astype(jnp.int32)

    def keep_going(state):
        kb, flag = state
        return jnp.logical_and(kb >= 0, flag > 0)

    def body(state):
        kb, _ = state
        for pr, cols in enumerate(pairs):
            carry = carry_ref[pr]
            pv, tot = _sweep(stacked_q(cols), cached(kc_ref, kb, cols), cached(vc_ref, kb, cols),
                             tri, carry, False)
            acc_ref[pr] += pv
            carry_ref[pr] = carry + tot
        return kb - 1, live()

    lax.while_loop(keep_going, body, (qi - 2, live()))

    for pr, cols in enumerate(pairs):
        acc = acc_ref[pr]
        yb_ref[:, cols] = jnp.where(lane < SB_HEAD_DIM, acc[:Q_BLOCK], acc[Q_BLOCK:]).astype(BF16)


def _mix(x2d, gain, w_in, seq):
    t = x2d.shape[0]
    jj = lax.broadcasted_iota(jnp.int32, (K_BLOCK, K_BLOCK), 0)
    ss = lax.broadcasted_iota(jnp.int32, (K_BLOCK, K_BLOCK), 1)
    tri = (jj > ss).astype(BF16)
    return pl.pallas_call(
        functools.partial(_mix_kernel, seq),
        grid=(t // Q_BLOCK,),
        in_specs=[
            _rows(Q_BLOCK, D_MODEL),
            _resident((1, D_MODEL)),
            _resident((D_MODEL, D_IN)),
            _resident((K_BLOCK, K_BLOCK)),
        ],
        out_specs=[
            _rows(Q_BLOCK, D_POOL),
            _rows(Q_BLOCK, 2 * D_MODEL),
            _rows(Q_BLOCK, D_SB),
        ],
        out_shape=[
            jax.ShapeDtypeStruct((t, D_POOL), F32),
            jax.ShapeDtypeStruct((t, 2 * D_MODEL), F32),
            jax.ShapeDtypeStruct((t, D_SB), BF16),
        ],
        scratch_shapes=[
            pltpu.VMEM((seq, D_SB), BF16),
            pltpu.VMEM((seq, D_SB), BF16),
            pltpu.VMEM((N_PAIRS, 2 * Q_BLOCK, LANES), F32),
            pltpu.VMEM((N_PAIRS, 2 * Q_BLOCK, 1), F32),
        ],
        compiler_params=pltpu.CompilerParams(
            dimension_semantics=("arbitrary",), vmem_limit_bytes=VMEM_LIMIT),
        name="mix",
    )(x2d, gain, w_in, tri)


def _window_sum(ubuf_ref, lvl_ref, cols, w):
    n = POOL_HALO + POST_TILE
    cur = ubuf_ref[SUBLANES:SUBLANES + n, cols] + ubuf_ref[SUBLANES - 1:SUBLANES - 1 + n, cols]
    k, slot = 2, 0
    while k < w:
        lvl_ref[slot, SUBLANES:SUBLANES + n, :] = cur
        cur = cur + lvl_ref[slot, SUBLANES - k:SUBLANES - k + n, :]
        k, slot = 2 * k, 1 - slot
    return cur[POOL_HALO:]


def _post_kernel(seq, u_ref, halo_ref, yb_ref, g_ref, x_ref, p_ref,
                 wpool_ref, pscale_ref, wa_ref, wb_ref, wout_ref,
                 gffn_ref, wgate_ref, wup_ref, wdown_ref,
                 gple_ref, wpg_ref, wpp_ref, gfin_ref, o_ref, ubuf_ref, lvl_ref):
    i = pl.program_id(0)
    t0 = (i * POST_TILE) % seq
    u = u_ref[...]
    halo = halo_ref[...]
    ubuf_ref[0:SUBLANES, :] = jnp.zeros((SUBLANES, D_POOL), F32)
    lvl_ref[:, 0:SUBLANES, :] = jnp.zeros((2, SUBLANES, POOL_GROUP_DIM), F32)
    ubuf_ref[SUBLANES:SUBLANES + POOL_HALO, :] = jnp.where(t0 == 0, jnp.zeros_like(halo), halo)
    ubuf_ref[SUBLANES + POOL_HALO:, :] = u
    pos = t0 + lax.broadcasted_iota(jnp.int32, (POST_TILE, 1), 0)
    mixed = []
    for g, w in enumerate(POOL_WINDOWS):
        cols = slice(g * POOL_GROUP_DIM, (g + 1) * POOL_GROUP_DIM)
        count = jnp.minimum(pos + 1, w).astype(F32)
        pooled = _window_sum(ubuf_ref, lvl_ref, cols, w) / count - u[:, cols]
        mixed.append(_dot(pooled.astype(BF16), wpool_ref[g]))
    y_a = jnp.concatenate(mixed, axis=1) * pscale_ref[...]

    g_all = g_ref[...]
    merged = (jax.nn.sigmoid(g_all[:, :D_MODEL]) * _dot(y_a.astype(BF16), wa_ref[...])
              + jax.nn.sigmoid(g_all[:, D_MODEL:]) * _dot(yb_ref[...], wb_ref[...]))
    x = x_ref[...] + _dot(merged.astype(BF16), wout_ref[...])

    h = _rms(x, gffn_ref[...]).astype(BF16)
    act = jax.nn.silu(_dot(h, wgate_ref[...])) * _dot(h, wup_ref[...])
    x = x + _dot(act.astype(BF16), wdown_ref[...])

    h = _rms(x, gple_ref[...]).astype(BF16)
    gate = jax.nn.sigmoid(_dot(h, wpg_ref[...]))
    x = x + gate * _dot(p_ref[...].astype(BF16), wpp_ref[...])
    o_ref[...] = _rms(x, gfin_ref[...])


def _post(u, y_b, g, x2d, p2d, seq, w_pool, pool_scale, w_a, w_b, w_out,
          gain_ffn, w_gate, w_up, w_down, gain_ple, w_pg, w_pp, gain_final):
    t = x2d.shape[0]
    halo_per_tile = POST_TILE // POOL_HALO
    weights = (w_pool, pool_scale, w_a, w_b, w_out, gain_ffn, w_gate, w_up, w_down,
               gain_ple, w_pg, w_pp, gain_final)
    buf_rows = SUBLANES + POOL_HALO + POST_TILE
    return pl.pallas_call(
        functools.partial(_post_kernel, seq),
        grid=(t // POST_TILE,),
        in_specs=[
            _rows(POST_TILE, D_POOL),
            pl.BlockSpec((POOL_HALO, D_POOL), lambda i: (jnp.maximum(i * halo_per_tile - 1, 0), 0)),
            _rows(POST_TILE, D_SB),
            _rows(POST_TILE, 2 * D_MODEL),
            _rows(POST_TILE, D_MODEL),
            _rows(POST_TILE, PLE_DIM),
        ] + [_resident(w.shape) for w in weights],
        out_specs=_rows(POST_TILE, D_MODEL),
        out_shape=jax.ShapeDtypeStruct((t, D_MODEL), F32),
        scratch_shapes=[
            pltpu.VMEM((buf_rows, D_POOL), F32),
            pltpu.VMEM((2, buf_rows, POOL_GROUP_DIM), F32),
        ],
        compiler_params=pltpu.CompilerParams(
            dimension_semantics=("arbitrary",), vmem_limit_bytes=VMEM_LIMIT),
        name="post",
    )(u, u, y_b, g, x2d, p2d, *weights)


def kernel(x, p, norm_mix, w_in, w_pool, pool_scale, w_branch_a, w_branch_b, w_out, norm_ffn,
           w_ffn_gate, w_ffn_up, w_ffn_down, norm_ple, w_ple_gate, w_ple_proj, norm_final):
    batch, seq, d_model = x.shape
    assert w_in.shape[0] == 1 and d_model == D_MODEL
    assert seq % POST_TILE == 0 and seq % Q_BLOCK == 0
    t = batch * seq
    x2d = x.reshape(t, D_MODEL)
    row = lambda v: v.reshape(1, -1).astype(F32)
    bf = lambda w: w[0].astype(BF16)
    u, g, y_b = _mix(x2d, row(norm_mix[0]), bf(w_in), seq)
    out = _post(u, y_b, g, x2d, p[0].reshape(t, PLE_DIM), seq,
                bf(w_pool), row(pool_scale[0]), bf(w_branch_a), bf(w_branch_b), bf(w_out),
                row(norm_ffn[0]), bf(w_ffn_gate), bf(w_ffn_up), bf(w_ffn_down),
                row(norm_ple[0]), bf(w_ple_gate), bf(w_ple_proj), row(norm_final))
    return out.reshape(batch, seq, D_MODEL)
```

```python
import functools

import jax
import jax.numpy as jnp
from jax import lax
from jax.experimental import pallas as pl
from jax.experimental.pallas import tpu as pltpu

D_MODEL = 1024
PLE_DIM = 256
D_POOL = D_MODEL // 2
POOL_WINDOWS = (2, 4, 8, 16)
POOL_GROUP_DIM = D_POOL // len(POOL_WINDOWS)
POOL_HALO = 16
SB_HEADS = 8
SB_HEAD_DIM = 64
D_SB = SB_HEADS * SB_HEAD_DIM
D_FF = ((8 * D_MODEL // 3 + 255) // 256) * 256
D_IN = D_POOL + 3 * D_SB + 2 * D_MODEL
RMS_EPS = 1e-6

LANES = 128
SUBLANES = 8
N_PAIRS = D_SB // LANES
POST_TILE = 512
POST_CHUNK = 256
Q_BLOCK = 256
K_BLOCK = Q_BLOCK
PIECE = 256
VMEM_LIMIT = 56 * 1024 * 1024
EXIT_LOG = -104.0

LOG2E = 1.4426950408889634

F32 = jnp.float32
BF16 = jnp.bfloat16


def _dot(a, b):
    return jnp.dot(a, b, preferred_element_type=F32)


def _rms(xf, gain_row):
    inv = lax.rsqrt(jnp.mean(xf * xf, axis=-1, keepdims=True) + RMS_EPS)
    return xf * inv * gain_row


def _resident(shape):
    zeros = (0,) * len(shape)
    return pl.BlockSpec(shape, lambda *_: zeros, pipeline_mode=pl.Buffered(1))


def _rows(tile, width):
    return pl.BlockSpec((tile, width), lambda i: (i, 0))


def _sweep(qs, kblk, vblk, tri, carry, diagonal, filler=None):
    rows = qs.shape[0]
    nz = lax.dot_general(qs, kblk, (((1,), (1,)), ((), ())),
                         preferred_element_type=F32)
    lf = jnp.minimum(nz, 0.0) - jnp.log(1.0 + jnp.exp2(jnp.abs(nz) * (-LOG2E)))
    if diagonal:
        t_loc = lax.broadcasted_iota(jnp.int32, (rows, K_BLOCK), 0) & (Q_BLOCK - 1)
        s_loc = lax.broadcasted_iota(jnp.int32, (rows, K_BLOCK), 1)
        mask = s_loc < t_loc
        lf = jnp.where(mask, lf, 0.0)
    if filler is not None:
        filler()
    suffix = _dot(lf.astype(BF16), tri)
    a = jnp.exp((lf - nz) + suffix)
    if diagonal:
        a = jnp.where(mask, a, 0.0)
    if filler is not None:
        filler()
    pv = _dot(a.astype(BF16), vblk)
    if carry is not None:
        pv = pv * jnp.exp(carry)
    return pv, jnp.sum(lf, axis=1, keepdims=True)


def _mix_kernel(seq, x_ref, gain_ref, w_ref, tri_ref, u_ref, g_ref, yb_ref,
                kc_ref, vc_ref, acc_ref, carry_ref):
    qi = lax.rem(pl.program_id(0), seq // Q_BLOCK)
    row0 = pl.multiple_of(qi * Q_BLOCK, Q_BLOCK)
    h = _rms(x_ref[...], gain_ref[...]).astype(BF16)
    q = (_dot(h, w_ref[:, D_POOL:D_POOL + D_SB]) * (-1.0 / 8.0)).astype(BF16)
    k = _dot(h, w_ref[:, D_POOL + D_SB:D_POOL + 2 * D_SB]).astype(BF16)
    v = _dot(h, w_ref[:, D_POOL + 2 * D_SB:D_POOL + 3 * D_SB]).astype(BF16)
    kc_ref[pl.ds(row0, Q_BLOCK), :] = k
    vc_ref[pl.ds(row0, Q_BLOCK), :] = v

    lane = lax.broadcasted_iota(jnp.int32, (Q_BLOCK, LANES), 1)
    tri = tri_ref[...]
    pairs = [slice(pr * LANES, (pr + 1) * LANES) for pr in range(N_PAIRS)]

    def stacked_q(cols):
        q2 = q[:, cols]
        zero = jnp.zeros_like(q2)
        return jnp.concatenate([jnp.where(lane < SB_HEAD_DIM, q2, zero),
                                jnp.where(lane >= SB_HEAD_DIM, q2, zero)], axis=0)

    def cached(ref, kb, cols):
        return ref[pl.ds(pl.multiple_of(kb * K_BLOCK, K_BLOCK), K_BLOCK), cols]

    has_prev = qi > 0
    prev = jnp.maximum(qi - 1, 0)

    pieces = [(u_ref, c, c) for c in range(0, D_POOL, PIECE)]
    pieces += [(g_ref, c, D_POOL + 3 * D_SB + c) for c in range(0, 2 * D_MODEL, PIECE)]

    def project_pieces(n):
        for _ in range(min(n, len(pieces))):
            ref, c, wc = pieces.pop(0)
            ref[:, c:c + PIECE] = _dot(h, w_ref[:, wc:wc + PIECE])

    for pr, cols in enumerate(pairs):
        qs = stacked_q(cols)
        fill = functools.partial(project_pieces, 1)
        pv_d, tot_d = _sweep(qs, k[:, cols], v[:, cols], tri, None, True, fill)
        pv_p, tot_p = _sweep(qs, cached(kc_ref, prev, cols), cached(vc_ref, prev, cols), tri,
                             tot_d, False, fill)
        acc_ref[pr] = pv_d + jnp.where(has_prev, pv_p, 0.0)
        carry_ref[pr] = tot_d + jnp.where(has_prev, tot_p, 0.0)
    project_pieces(len(pieces))

    def live():
        return (jnp.max(carry_ref[...]) >= EXIT_LOG).astype(jnp.int32)

    def keep_going(state):
        kb, flag = state
        return jnp.logical_and(kb >= 0, flag > 0)

    def body(state):
        kb, _ = state
        for pr, cols in enumerate(pairs):
            carry = carry_ref[pr]
            pv, tot = _sweep(stacked_q(cols), cached(kc_ref, kb, cols), cached(vc_ref, kb, cols),
                             tri, carry, False)
            acc_ref[pr] += pv
            carry_ref[pr] = carry + tot
        return kb - 1, live()

    lax.while_loop(keep_going, body, (qi - 2, live()))

    for pr, cols in enumerate(pairs):
        acc = acc_ref[pr]
        yb_ref[:, cols] = jnp.where(lane < SB_HEAD_DIM, acc[:Q_BLOCK], acc[Q_BLOCK:]).astype(BF16)


def _mix(x2d, gain, w_in, seq):
    t = x2d.shape[0]
    jj = lax.broadcasted_iota(jnp.int32, (K_BLOCK, K_BLOCK), 0)
    ss = lax.broadcasted_iota(jnp.int32, (K_BLOCK, K_BLOCK), 1)
    tri = (jj > ss).astype(BF16)
    return pl.pallas_call(
        functools.partial(_mix_kernel, seq),
        grid=(t // Q_BLOCK,),
        in_specs=[
            _rows(Q_BLOCK, D_MODEL),
            _resident((1, D_MODEL)),
            _resident((D_MODEL, D_IN)),
            _resident((K_BLOCK, K_BLOCK)),
        ],
        out_specs=[
            _rows(Q_BLOCK, D_POOL),
            _rows(Q_BLOCK, 2 * D_MODEL),
            _rows(Q_BLOCK, D_SB),
        ],
        out_shape=[
            jax.ShapeDtypeStruct((t, D_POOL), F32),
            jax.ShapeDtypeStruct((t, 2 * D_MODEL), F32),
            jax.ShapeDtypeStruct((t, D_SB), BF16),
        ],
        scratch_shapes=[
            pltpu.VMEM((seq, D_SB), BF16),
            pltpu.VMEM((seq, D_SB), BF16),
            pltpu.VMEM((N_PAIRS, 2 * Q_BLOCK, LANES), F32),
            pltpu.VMEM((N_PAIRS, 2 * Q_BLOCK, 1), F32),
        ],
        compiler_params=pltpu.CompilerParams(
            dimension_semantics=("arbitrary",), vmem_limit_bytes=VMEM_LIMIT),
        name="mix",
    )(x2d, gain, w_in, tri)


def _window_sum(ubuf_ref, lvl_ref, cols, w):
    n = POOL_HALO + POST_TILE
    cur = ubuf_ref[SUBLANES:SUBLANES + n, cols] + ubuf_ref[SUBLANES - 1:SUBLANES - 1 + n, cols]
    k, slot = 2, 0
    while k < w:
        lvl_ref[slot, SUBLANES:SUBLANES + n, :] = cur
        cur = cur + lvl_ref[slot, SUBLANES - k:SUBLANES - k + n, :]
        k, slot = 2 * k, 1 - slot
    return cur[POOL_HALO:]


def _post_kernel(seq, u_ref, halo_ref, yb_ref, g_ref, x_ref, p_ref,
                 wpool_ref, pscale_ref, wa_ref, wb_ref, wout_ref,
                 gffn_ref, wgate_ref, wup_ref, wdown_ref,
                 gple_ref, wpg_ref, wpp_ref, gfin_ref, o_ref, ubuf_ref, lvl_ref):
    i = pl.program_id(0)
    t0 = (i * POST_TILE) % seq
    u = u_ref[...]
    halo = halo_ref[...]
    ubuf_ref[0:SUBLANES, :] = jnp.zeros((SUBLANES, D_POOL), F32)
    lvl_ref[:, 0:SUBLANES, :] = jnp.zeros((2, SUBLANES, POOL_GROUP_DIM), F32)
    ubuf_ref[SUBLANES:SUBLANES + POOL_HALO, :] = jnp.where(t0 == 0, jnp.zeros_like(halo), halo)
    ubuf_ref[SUBLANES + POOL_HALO:, :] = u
    pos = t0 + lax.broadcasted_iota(jnp.int32, (POST_TILE, 1), 0)
    pooled = []
    for g, w in enumerate(POOL_WINDOWS):
        cols = slice(g * POOL_GROUP_DIM, (g + 1) * POOL_GROUP_DIM)
        count = jnp.minimum(pos + 1, w).astype(F32)
        pooled.append((_window_sum(ubuf_ref, lvl_ref, cols, w) / count - u[:, cols]).astype(BF16))

    chunks = [slice(r, r + POST_CHUNK) for r in range(0, POST_TILE, POST_CHUNK)]
    groups = range(len(POOL_WINDOWS))

    y_a = [(jnp.concatenate([_dot(pooled[g][rows], wpool_ref[g]) for g in groups], axis=1)
            * pscale_ref[...]).astype(BF16) for rows in chunks]
    merged = [(jax.nn.sigmoid(g_ref[rows, 0:D_MODEL]) * _dot(y_a[c], wa_ref[...])
               + jax.nn.sigmoid(g_ref[rows, D_MODEL:2 * D_MODEL]) * _dot(yb_ref[rows, :], wb_ref[...])
               ).astype(BF16) for c, rows in enumerate(chunks)]
    x1 = [x_ref[rows, :] + _dot(merged[c], wout_ref[...]) for c, rows in enumerate(chunks)]
    h1 = [_rms(x, gffn_ref[...]).astype(BF16) for x in x1]
    act = [(jax.nn.silu(_dot(h, wgate_ref[...])) * _dot(h, wup_ref[...])).astype(BF16) for h in h1]
    x2 = [x + _dot(a, wdown_ref[...]) for x, a in zip(x1, act)]
    h2 = [_rms(x, gple_ref[...]).astype(BF16) for x in x2]
    for c, rows in enumerate(chunks):
        gate = jax.nn.sigmoid(_dot(h2[c], wpg_ref[...]))
        x3 = x2[c] + gate * _dot(p_ref[rows, :].astype(BF16), wpp_ref[...])
        o_ref[rows, :] = _rms(x3, gfin_ref[...])


def _post(u, y_b, g, x2d, p2d, seq, w_pool, pool_scale, w_a, w_b, w_out,
          gain_ffn, w_gate, w_up, w_down, gain_ple, w_pg, w_pp, gain_final):
    t = x2d.shape[0]
    halo_per_tile = POST_TILE // POOL_HALO
    weights = (w_pool, pool_scale, w_a, w_b, w_out, gain_ffn, w_gate, w_up, w_down,
               gain_ple, w_pg, w_pp, gain_final)
    buf_rows = SUBLANES + POOL_HALO + POST_TILE
    return pl.pallas_call(
        functools.partial(_post_kernel, seq),
        grid=(t // POST_TILE,),
        in_specs=[
            _rows(POST_TILE, D_POOL),
            pl.BlockSpec((POOL_HALO, D_POOL), lambda i: (jnp.maximum(i * halo_per_tile - 1, 0), 0)),
            _rows(POST_TILE, D_SB),
            _rows(POST_TILE, 2 * D_MODEL),
            _rows(POST_TILE, D_MODEL),
            _rows(POST_TILE, PLE_DIM),
        ] + [_resident(w.shape) for w in weights],
        out_specs=_rows(POST_TILE, D_MODEL),
        out_shape=jax.ShapeDtypeStruct((t, D_MODEL), F32),
        scratch_shapes=[
            pltpu.VMEM((buf_rows, D_POOL), F32),
            pltpu.VMEM((2, buf_rows, POOL_GROUP_DIM), F32),
        ],
        compiler_params=pltpu.CompilerParams(
            dimension_semantics=("arbitrary",), vmem_limit_bytes=VMEM_LIMIT),
        name="post",
    )(u, u, y_b, g, x2d, p2d, *weights)


def kernel(x, p, norm_mix, w_in, w_pool, pool_scale, w_branch_a, w_branch_b, w_out, norm_ffn,
           w_ffn_gate, w_ffn_up, w_ffn_down, norm_ple, w_ple_gate, w_ple_proj, norm_final):
    batch, seq, d_model = x.shape
    assert w_in.shape[0] == 1 and d_model == D_MODEL
    assert seq % POST_TILE == 0 and seq % Q_BLOCK == 0
    t = batch * seq
    x2d = x.reshape(t, D_MODEL)
    row = lambda v: v.reshape(1, -1).astype(F32)
    bf = lambda w: w[0].astype(BF16)
    u, g, y_b = _mix(x2d, row(norm_mix[0]), bf(w_in), seq)
    out = _post(u, y_b, g, x2d, p[0].reshape(t, PLE_DIM), seq,
                bf(w_pool), row(pool_scale[0]), bf(w_branch_a), bf(w_branch_b), bf(w_out),
                row(norm_ffn[0]), bf(w_ffn_gate), bf(w_ffn_up), bf(w_ffn_down),
                row(norm_ple[0]), bf(w_ple_gate), bf(w_ple_proj), row(norm_final))
    return out.reshape(batch, seq, D_MODEL)
```

```python
import functools

import jax
import jax.numpy as jnp
from jax import lax
from jax.experimental import pallas as pl
from jax.experimental.pallas import tpu as pltpu

D_MODEL = 1024
PLE_DIM = 256
D_POOL = D_MODEL // 2
POOL_WINDOWS = (2, 4, 8, 16)
POOL_GROUP_DIM = D_POOL // len(POOL_WINDOWS)
POOL_HALO = 16
SB_HEADS = 8
SB_HEAD_DIM = 64
D_SB = SB_HEADS * SB_HEAD_DIM
D_FF = ((8 * D_MODEL // 3 + 255) // 256) * 256
D_IN = D_POOL + 3 * D_SB + 2 * D_MODEL
RMS_EPS = 1e-6

LANES = 128
SUBLANES = 8
N_PAIRS = D_SB // LANES
POST_TILE = 512
POST_CHUNK = 256
Q_BLOCK = 256
K_BLOCK = Q_BLOCK
PIECE = 256
POST_VMEM_LIMIT = 56 * 1024 * 1024
MIX_VMEM_LIMIT = 44 * 1024 * 1024
EXIT_LOG = -104.0

LOG2E = 1.4426950408889634

F32 = jnp.float32
BF16 = jnp.bfloat16


def _dot(a, b):
    return jnp.dot(a, b, preferred_element_type=F32)


def _rms(xf, gain_row):
    inv = lax.rsqrt(jnp.mean(xf * xf, axis=-1, keepdims=True) + RMS_EPS)
    return xf * inv * gain_row


def _resident(shape):
    zeros = (0,) * len(shape)
    return pl.BlockSpec(shape, lambda *_: zeros, pipeline_mode=pl.Buffered(1))


def _rows(tile, width):
    return pl.BlockSpec((tile, width), lambda i: (i, 0))


def _sweep(qs, kblk, vblk, tri, carry, diagonal, filler=None):
    rows = qs.shape[0]
    nz = lax.dot_general(qs, kblk, (((1,), (1,)), ((), ())),
                         preferred_element_type=F32)
    lf = jnp.minimum(nz, 0.0) - jnp.log(1.0 + jnp.exp2(jnp.abs(nz) * (-LOG2E)))
    if diagonal:
        t_loc = lax.broadcasted_iota(jnp.int32, (rows, K_BLOCK), 0) & (Q_BLOCK - 1)
        s_loc = lax.broadcasted_iota(jnp.int32, (rows, K_BLOCK), 1)
        mask = s_loc < t_loc
        lf = jnp.where(mask, lf, 0.0)
    if filler is not None:
        filler()
    suffix = _dot(lf.astype(BF16), tri)
    a = jnp.exp((lf - nz) + suffix)
    if diagonal:
        a = jnp.where(mask, a, 0.0)
    if filler is not None:
        filler()
    pv = _dot(a.astype(BF16), vblk)
    if carry is not None:
        pv = pv * jnp.exp(carry)
    return pv, jnp.sum(lf, axis=1, keepdims=True)


def _mix_kernel(seq, x_ref, gain_ref, w_ref, tri_ref, u_ref, g_ref, yb_ref,
                kc_ref, vc_ref, acc_ref, carry_ref):
    qi = lax.rem(pl.program_id(0), seq // Q_BLOCK)
    row0 = pl.multiple_of(qi * Q_BLOCK, Q_BLOCK)
    h = _rms(x_ref[...], gain_ref[...]).astype(BF16)
    q = (_dot(h, w_ref[:, D_POOL:D_POOL + D_SB]) * (-1.0 / 8.0)).astype(BF16)
    k = _dot(h, w_ref[:, D_POOL + D_SB:D_POOL + 2 * D_SB]).astype(BF16)
    v = _dot(h, w_ref[:, D_POOL + 2 * D_SB:D_POOL + 3 * D_SB]).astype(BF16)
    kc_ref[pl.ds(row0, Q_BLOCK), :] = k
    vc_ref[pl.ds(row0, Q_BLOCK), :] = v

    lane = lax.broadcasted_iota(jnp.int32, (Q_BLOCK, LANES), 1)
    tri = tri_ref[...]
    pairs = [slice(pr * LANES, (pr + 1) * LANES) for pr in range(N_PAIRS)]

    def stacked_q(cols):
        q2 = q[:, cols]
        zero = jnp.zeros_like(q2)
        return jnp.concatenate([jnp.where(lane < SB_HEAD_DIM, q2, zero),
                                jnp.where(lane >= SB_HEAD_DIM, q2, zero)], axis=0)

    def cached(ref, kb, cols):
        return ref[pl.ds(pl.multiple_of(kb * K_BLOCK, K_BLOCK), K_BLOCK), cols]

    has_prev = qi > 0
    prev = jnp.maximum(qi - 1, 0)

    pieces = [(u_ref, c, c) for c in range(0, D_POOL, PIECE)]
    pieces += [(g_ref, c, D_POOL + 3 * D_SB + c) for c in range(0, 2 * D_MODEL, PIECE)]

    def project_pieces(n):
        for _ in range(min(n, len(pieces))):
            ref, c, wc = pieces.pop(0)
            ref[:, c:c + PIECE] = _dot(h, w_ref[:, wc:wc + PIECE])

    for pr, cols in enumerate(pairs):
        qs = stacked_q(cols)
        fill = functools.partial(project_pieces, 1)
        pv_d, tot_d = _sweep(qs, k[:, cols], v[:, cols], tri, None, True, fill)
        pv_p, tot_p = _sweep(qs, cached(kc_ref, prev, cols), cached(vc_ref, prev, cols), tri,
                             tot_d, False, fill)
        acc_ref[pr] = pv_d + jnp.where(has_prev, pv_p, 0.0)
        carry_ref[pr] = tot_d + jnp.where(has_prev, tot_p, 0.0)
    project_pieces(len(pieces))

    def live():
        return (jnp.max(carry_ref[...]) >= EXIT_LOG).astype(jnp.int32)

    def keep_going(state):
        kb, flag = state
        return jnp.logical_and(kb >= 0, flag > 0)

    def body(state):
        kb, _ = state
        for pr, cols in enumerate(pairs):
            carry = carry_ref[pr]
            pv, tot = _sweep(stacked_q(cols), cached(kc_ref, kb, cols), cached(vc_ref, kb, cols),
                             tri, carry, False)
            acc_ref[pr] += pv
            carry_ref[pr] = carry + tot
        return kb - 1, live()

    lax.while_loop(keep_going, body, (qi - 2, live()))

    for pr, cols in enumerate(pairs):
        acc = acc_ref[pr]
        yb_ref[:, cols] = jnp.where(lane < SB_HEAD_DIM, acc[:Q_BLOCK], acc[Q_BLOCK:]).astype(BF16)


def _mix(x2d, gain, w_in, seq):
    t = x2d.shape[0]
    jj = lax.broadcasted_iota(jnp.int32, (K_BLOCK, K_BLOCK), 0)
    ss = lax.broadcasted_iota(jnp.int32, (K_BLOCK, K_BLOCK), 1)
    tri = (jj > ss).astype(BF16)
    return pl.pallas_call(
        functools.partial(_mix_kernel, seq),
        grid=(t // Q_BLOCK,),
        in_specs=[
            _rows(Q_BLOCK, D_MODEL),
            _resident((1, D_MODEL)),
            _resident((D_MODEL, D_IN)),
            _resident((K_BLOCK, K_BLOCK)),
        ],
        out_specs=[
            _rows(Q_BLOCK, D_POOL),
            _rows(Q_BLOCK, 2 * D_MODEL),
            _rows(Q_BLOCK, D_SB),
        ],
        out_shape=[
            jax.ShapeDtypeStruct((t, D_POOL), F32),
            jax.ShapeDtypeStruct((t, 2 * D_MODEL), F32),
            jax.ShapeDtypeStruct((t, D_SB), BF16),
        ],
        scratch_shapes=[
            pltpu.VMEM((seq, D_SB), BF16),
            pltpu.VMEM((seq, D_SB), BF16),
            pltpu.VMEM((N_PAIRS, 2 * Q_BLOCK, LANES), F32),
            pltpu.VMEM((N_PAIRS, 2 * Q_BLOCK, 1), F32),
        ],
        compiler_params=pltpu.CompilerParams(
            dimension_semantics=("arbitrary",), vmem_limit_bytes=MIX_VMEM_LIMIT),
        name="mix",
    )(x2d, gain, w_in, tri)


def _window_sum(ubuf_ref, lvl_ref, cols, w):
    n = POOL_HALO + POST_TILE
    cur = ubuf_ref[SUBLANES:SUBLANES + n, cols] + ubuf_ref[SUBLANES - 1:SUBLANES - 1 + n, cols]
    k, slot = 2, 0
    while k < w:
        lvl_ref[slot, SUBLANES:SUBLANES + n, :] = cur
        cur = cur + lvl_ref[slot, SUBLANES - k:SUBLANES - k + n, :]
        k, slot = 2 * k, 1 - slot
    return cur[POOL_HALO:]


def _post_kernel(seq, u_ref, halo_ref, yb_ref, g_ref, x_ref, p_ref,
                 wpool_ref, pscale_ref, wa_ref, wb_ref, wout_ref,
                 gffn_ref, wgate_ref, wup_ref, wdown_ref,
                 gple_ref, wpg_ref, wpp_ref, gfin_ref, o_ref, ubuf_ref, lvl_ref):
    i = pl.program_id(0)
    t0 = (i * POST_TILE) % seq
    u = u_ref[...]
    halo = halo_ref[...]
    ubuf_ref[0:SUBLANES, :] = jnp.zeros((SUBLANES, D_POOL), F32)
    lvl_ref[:, 0:SUBLANES, :] = jnp.zeros((2, SUBLANES, POOL_GROUP_DIM), F32)
    ubuf_ref[SUBLANES:SUBLANES + POOL_HALO, :] = jnp.where(t0 == 0, jnp.zeros_like(halo), halo)
    ubuf_ref[SUBLANES + POOL_HALO:, :] = u
    pos = t0 + lax.broadcasted_iota(jnp.int32, (POST_TILE, 1), 0)
    pooled = []
    for g, w in enumerate(POOL_WINDOWS):
        cols = slice(g * POOL_GROUP_DIM, (g + 1) * POOL_GROUP_DIM)
        count = jnp.minimum(pos + 1, w).astype(F32)
        pooled.append((_window_sum(ubuf_ref, lvl_ref, cols, w) / count - u[:, cols]).astype(BF16))

    chunks = [slice(r, r + POST_CHUNK) for r in range(0, POST_TILE, POST_CHUNK)]
    groups = range(len(POOL_WINDOWS))

    y_a = [(jnp.concatenate([_dot(pooled[g][rows], wpool_ref[g]) for g in groups], axis=1)
            * pscale_ref[...]).astype(BF16) for rows in chunks]
    merged = [(jax.nn.sigmoid(g_ref[rows, 0:D_MODEL]) * _dot(y_a[c], wa_ref[...])
               + jax.nn.sigmoid(g_ref[rows, D_MODEL:2 * D_MODEL]) * _dot(yb_ref[rows, :], wb_ref[...])
               ).astype(BF16) for c, rows in enumerate(chunks)]
    x1 = [x_ref[rows, :] + _dot(merged[c], wout_ref[...]) for c, rows in enumerate(chunks)]
    h1 = [_rms(x, gffn_ref[...]).astype(BF16) for x in x1]
    act = [(jax.nn.silu(_dot(h, wgate_ref[...])) * _dot(h, wup_ref[...])).astype(BF16) for h in h1]
    x2 = [x + _dot(a, wdown_ref[...]) for x, a in zip(x1, act)]
    h2 = [_rms(x, gple_ref[...]).astype(BF16) for x in x2]
    for c, rows in enumerate(chunks):
        gate = jax.nn.sigmoid(_dot(h2[c], wpg_ref[...]))
        x3 = x2[c] + gate * _dot(p_ref[rows, :].astype(BF16), wpp_ref[...])
        o_ref[rows, :] = _rms(x3, gfin_ref[...])


def _post(u, y_b, g, x2d, p2d, seq, w_pool, pool_scale, w_a, w_b, w_out,
          gain_ffn, w_gate, w_up, w_down, gain_ple, w_pg, w_pp, gain_final):
    t = x2d.shape[0]
    halo_per_tile = POST_TILE // POOL_HALO
    weights = (w_pool, pool_scale, w_a, w_b, w_out, gain_ffn, w_gate, w_up, w_down,
               gain_ple, w_pg, w_pp, gain_final)
    buf_rows = SUBLANES + POOL_HALO + POST_TILE
    return pl.pallas_call(
        functools.partial(_post_kernel, seq),
        grid=(t // POST_TILE,),
        in_specs=[
            _rows(POST_TILE, D_POOL),
            pl.BlockSpec((POOL_HALO, D_POOL), lambda i: (jnp.maximum(i * halo_per_tile - 1, 0), 0)),
            _rows(POST_TILE, D_SB),
            _rows(POST_TILE, 2 * D_MODEL),
            _rows(POST_TILE, D_MODEL),
            _rows(POST_TILE, PLE_DIM),
        ] + [_resident(w.shape) for w in weights],
        out_specs=_rows(POST_TILE, D_MODEL),
        out_shape=jax.ShapeDtypeStruct((t, D_MODEL), F32),
        scratch_shapes=[
            pltpu.VMEM((buf_rows, D_POOL), F32),
            pltpu.VMEM((2, buf_rows, POOL_GROUP_DIM), F32),
        ],
        compiler_params=pltpu.CompilerParams(
            dimension_semantics=("arbitrary",), vmem_limit_bytes=POST_VMEM_LIMIT),
        name="post",
    )(u, u, y_b, g, x2d, p2d, *weights)


def kernel(x, p, norm_mix, w_in, w_pool, pool_scale, w_branch_a, w_branch_b, w_out, norm_ffn,
           w_ffn_gate, w_ffn_up, w_ffn_down, norm_ple, w_ple_gate, w_ple_proj, norm_final):
    batch, seq, d_model = x.shape
    assert w_in.shape[0] == 1 and d_model == D_MODEL
    assert seq % POST_TILE == 0 and seq % Q_BLOCK == 0
    t = batch * seq
    x2d = x.reshape(t, D_MODEL)
    row = lambda v: v.reshape(1, -1).astype(F32)
    bf = lambda w: w[0].astype(BF16)
    u, g, y_b = _mix(x2d, row(norm_mix[0]), bf(w_in), seq)
    out = _post(u, y_b, g, x2d, p[0].reshape(t, PLE_DIM), seq,
                bf(w_pool), row(pool_scale[0]), bf(w_branch_a), bf(w_branch_b), bf(w_out),
                row(norm_ffn[0]), bf(w_ffn_gate), bf(w_ffn_up), bf(w_ffn_down),
                row(norm_ple[0]), bf(w_ple_gate), bf(w_ple_proj), row(norm_final))
    return out.reshape(batch, seq, D_MODEL)
```

```python
import functools

import jax
import jax.numpy as jnp
from jax import lax
from jax.experimental import pallas as pl
from jax.experimental.pallas import tpu as pltpu

D_MODEL = 1024
PLE_DIM = 256
D_POOL = D_MODEL // 2
POOL_WINDOWS = (2, 4, 8, 16)
POOL_GROUP_DIM = D_POOL // len(POOL_WINDOWS)
POOL_HALO = 16
SB_HEADS = 8
SB_HEAD_DIM = 64
D_SB = SB_HEADS * SB_HEAD_DIM
D_FF = ((8 * D_MODEL // 3 + 255) // 256) * 256
D_IN = D_POOL + 3 * D_SB + 2 * D_MODEL
RMS_EPS = 1e-6

LANES = 128
SUBLANES = 8
N_PAIRS = D_SB // LANES
POST_TILE = 512
POST_CHUNK = 256
Q_BLOCK = 256
K_BLOCK = Q_BLOCK
PIECE = 256
FF_PARTS = (slice(0, 6 * PIECE), slice(6 * PIECE, D_FF))
POST_VMEM_LIMIT = 54 * 1024 * 1024
MIX_VMEM_LIMIT = 44 * 1024 * 1024
EXIT_LOG = -104.0

LOG2E = 1.4426950408889634

F32 = jnp.float32
BF16 = jnp.bfloat16


def _dot(a, b):
    return jnp.dot(a, b, preferred_element_type=F32)


def _rms(xf, gain_row):
    inv = lax.rsqrt(jnp.mean(xf * xf, axis=-1, keepdims=True) + RMS_EPS)
    return xf * inv * gain_row


def _resident(shape):
    zeros = (0,) * len(shape)
    return pl.BlockSpec(shape, lambda *_: zeros, pipeline_mode=pl.Buffered(1))


def _rows(tile, width):
    return pl.BlockSpec((tile, width), lambda i: (i, 0))


def _sweep(qs, kblk, vblk, tri, carry, diagonal, filler=None):
    rows = qs.shape[0]
    nz = lax.dot_general(qs, kblk, (((1,), (1,)), ((), ())),
                         preferred_element_type=F32)
    lf = jnp.minimum(nz, 0.0) - jnp.log(1.0 + jnp.exp2(jnp.abs(nz) * (-LOG2E)))
    if diagonal:
        t_loc = lax.broadcasted_iota(jnp.int32, (rows, K_BLOCK), 0) & (Q_BLOCK - 1)
        s_loc = lax.broadcasted_iota(jnp.int32, (rows, K_BLOCK), 1)
        mask = s_loc < t_loc
        lf = jnp.where(mask, lf, 0.0)
    if filler is not None:
        filler()
    suffix = _dot(lf.astype(BF16), tri)
    a = jnp.exp((lf - nz) + suffix)
    if diagonal:
        a = jnp.where(mask, a, 0.0)
    if filler is not None:
        filler()
    pv = _dot(a.astype(BF16), vblk)
    if carry is not None:
        pv = pv * jnp.exp(carry)
    return pv, jnp.sum(lf, axis=1, keepdims=True)


def _mix_kernel(seq, x_ref, gain_ref, w_ref, tri_ref, u_ref, g_ref, yb_ref,
                kc_ref, vc_ref, qst_ref, kst_ref, vst_ref, acc_ref, carry_ref):
    i = pl.program_id(0)
    n_blk = seq // Q_BLOCK

    @pl.when(i == 0)
    def _():
        kc_ref[...] = jnp.zeros_like(kc_ref)
        vc_ref[...] = jnp.zeros_like(vc_ref)
        qst_ref[...] = jnp.zeros_like(qst_ref)

    qi = lax.rem(i + (n_blk - 1), n_blk)
    q = qst_ref[lax.rem(i + 1, 2)]
    q_slot = lax.rem(i, 2)
    h = _rms(x_ref[...], gain_ref[...]).astype(BF16)

    def store_piece(c):
        acc = _dot(h, w_ref[:, c:c + PIECE])
        if c < D_POOL:
            u_ref[:, c:c + PIECE] = acc
        elif c < D_POOL + D_SB:
            qst_ref[q_slot, :, c - D_POOL:c - D_POOL + PIECE] = (acc * (-1.0 / 8.0)).astype(BF16)
        elif c < D_POOL + 2 * D_SB:
            kst_ref[:, c - D_POOL - D_SB:c - D_POOL - D_SB + PIECE] = acc.astype(BF16)
        elif c < D_POOL + 3 * D_SB:
            vst_ref[:, c - D_POOL - 2 * D_SB:c - D_POOL - 2 * D_SB + PIECE] = acc.astype(BF16)
        else:
            g_ref[:, c - D_POOL - 3 * D_SB:c - D_POOL - 3 * D_SB + PIECE] = acc

    pieces = list(range(0, D_IN, PIECE))

    def fill():
        if pieces:
            store_piece(pieces.pop(0))

    lane = lax.broadcasted_iota(jnp.int32, (Q_BLOCK, LANES), 1)
    tri = tri_ref[...]
    pairs = [slice(pr * LANES, (pr + 1) * LANES) for pr in range(N_PAIRS)]

    def stacked_q(cols):
        q2 = q[:, cols]
        zero = jnp.zeros_like(q2)
        return jnp.concatenate([jnp.where(lane < SB_HEAD_DIM, q2, zero),
                                jnp.where(lane >= SB_HEAD_DIM, q2, zero)], axis=0)

    def cached(ref, kb, cols):
        return ref[pl.ds(pl.multiple_of(kb * K_BLOCK, K_BLOCK), K_BLOCK), cols]

    has_prev = qi > 0
    prev = jnp.maximum(qi - 1, 0)
    for pr, cols in enumerate(pairs):
        qs = stacked_q(cols)
        pv_d, tot_d = _sweep(qs, cached(kc_ref, qi, cols), cached(vc_ref, qi, cols), tri,
                             None, True, fill)
        pv_p, tot_p = _sweep(qs, cached(kc_ref, prev, cols), cached(vc_ref, prev, cols), tri,
                             tot_d, False, fill)
        acc_ref[pr] = pv_d + jnp.where(has_prev, pv_p, 0.0)
        carry_ref[pr] = tot_d + jnp.where(has_prev, tot_p, 0.0)
    while pieces:
        fill()

    def live():
        return (jnp.max(carry_ref[...]) >= EXIT_LOG).astype(jnp.int32)

    def keep_going(state):
        kb, flag = state
        return jnp.logical_and(kb >= 0, flag > 0)

    def body(state):
        kb, _ = state
        for pr, cols in enumerate(pairs):
            carry = carry_ref[pr]
            pv, tot = _sweep(stacked_q(cols), cached(kc_ref, kb, cols), cached(vc_ref, kb, cols),
                             tri, carry, False)
            acc_ref[pr] += pv
            carry_ref[pr] = carry + tot
        return kb - 1, live()

    lax.while_loop(keep_going, body, (qi - 2, live()))

    for pr, cols in enumerate(pairs):
        acc = acc_ref[pr]
        yb_ref[:, cols] = jnp.where(lane < SB_HEAD_DIM, acc[:Q_BLOCK], acc[Q_BLOCK:]).astype(BF16)

    row0 = pl.multiple_of(lax.rem(i, n_blk) * Q_BLOCK, Q_BLOCK)
    kc_ref[pl.ds(row0, Q_BLOCK), :] = kst_ref[...]
    vc_ref[pl.ds(row0, Q_BLOCK), :] = vst_ref[...]


def _mix(x2d, gain, w_in, seq):
    t = x2d.shape[0]
    n_tiles = t // Q_BLOCK
    jj = lax.broadcasted_iota(jnp.int32, (K_BLOCK, K_BLOCK), 0)
    ss = lax.broadcasted_iota(jnp.int32, (K_BLOCK, K_BLOCK), 1)
    tri = (jj > ss).astype(BF16)
    this_tile = lambda i: (jnp.minimum(i, n_tiles - 1), 0)
    last_tile = lambda i: (jnp.maximum(i - 1, 0), 0)
    return pl.pallas_call(
        functools.partial(_mix_kernel, seq),
        grid=(n_tiles + 1,),
        in_specs=[
            pl.BlockSpec((Q_BLOCK, D_MODEL), this_tile),
            _resident((1, D_MODEL)),
            _resident((D_MODEL, D_IN)),
            _resident((K_BLOCK, K_BLOCK)),
        ],
        out_specs=[
            pl.BlockSpec((Q_BLOCK, D_POOL), this_tile),
            pl.BlockSpec((Q_BLOCK, 2 * D_MODEL), this_tile),
            pl.BlockSpec((Q_BLOCK, D_SB), last_tile),
        ],
        out_shape=[
            jax.ShapeDtypeStruct((t, D_POOL), F32),
            jax.ShapeDtypeStruct((t, 2 * D_MODEL), F32),
            jax.ShapeDtypeStruct((t, D_SB), BF16),
        ],
        scratch_shapes=[
            pltpu.VMEM((seq, D_SB), BF16),
            pltpu.VMEM((seq, D_SB), BF16),
            pltpu.VMEM((2, Q_BLOCK, D_SB), BF16),
            pltpu.VMEM((Q_BLOCK, D_SB), BF16),
            pltpu.VMEM((Q_BLOCK, D_SB), BF16),
            pltpu.VMEM((N_PAIRS, 2 * Q_BLOCK, LANES), F32),
            pltpu.VMEM((N_PAIRS, 2 * Q_BLOCK, 1), F32),
        ],
        compiler_params=pltpu.CompilerParams(
            dimension_semantics=("arbitrary",), vmem_limit_bytes=MIX_VMEM_LIMIT),
        name="mix",
    )(x2d, gain, w_in, tri)


def _window_sum(ubuf_ref, lvl_ref, cols, w):
    n = POOL_HALO + POST_TILE
    cur = ubuf_ref[SUBLANES:SUBLANES + n, cols] + ubuf_ref[SUBLANES - 1:SUBLANES - 1 + n, cols]
    k, slot = 2, 0
    while k < w:
        lvl_ref[slot, SUBLANES:SUBLANES + n, :] = cur
        cur = cur + lvl_ref[slot, SUBLANES - k:SUBLANES - k + n, :]
        k, slot = 2 * k, 1 - slot
    return cur[POOL_HALO:]


def _post_kernel(seq, u_ref, halo_ref, yb_ref, g_ref, x_ref, p_ref,
                 wpool_ref, pscale_ref, wa_ref, wb_ref, wout_ref,
                 gffn_ref, wgate_ref, wup_ref, wdown_ref,
                 gple_ref, wpg_ref, wpp_ref, gfin_ref, o_ref, ubuf_ref, lvl_ref):
    i = pl.program_id(0)
    t0 = (i * POST_TILE) % seq
    u = u_ref[...]
    halo = halo_ref[...]
    ubuf_ref[0:SUBLANES, :] = jnp.zeros((SUBLANES, D_POOL), F32)
    lvl_ref[:, 0:SUBLANES, :] = jnp.zeros((2, SUBLANES, POOL_GROUP_DIM), F32)
    ubuf_ref[SUBLANES:SUBLANES + POOL_HALO, :] = jnp.where(t0 == 0, jnp.zeros_like(halo), halo)
    ubuf_ref[SUBLANES + POOL_HALO:, :] = u
    pos = t0 + lax.broadcasted_iota(jnp.int32, (POST_TILE, 1), 0)
    pooled = []
    for g, w in enumerate(POOL_WINDOWS):
        cols = slice(g * POOL_GROUP_DIM, (g + 1) * POOL_GROUP_DIM)
        count = jnp.minimum(pos + 1, w).astype(F32)
        pooled.append((_window_sum(ubuf_ref, lvl_ref, cols, w) / count - u[:, cols]).astype(BF16))

    chunks = [slice(r, r + POST_CHUNK) for r in range(0, POST_TILE, POST_CHUNK)]
    groups = range(len(POOL_WINDOWS))

    y_a = [(jnp.concatenate([_dot(pooled[g][rows], wpool_ref[g]) for g in groups], axis=1)
            * pscale_ref[...]).astype(BF16) for rows in chunks]
    merged = [(jax.nn.sigmoid(g_ref[rows, 0:D_MODEL]) * _dot(y_a[c], wa_ref[...])
               + jax.nn.sigmoid(g_ref[rows, D_MODEL:2 * D_MODEL]) * _dot(yb_ref[rows, :], wb_ref[...])
               ).astype(BF16) for c, rows in enumerate(chunks)]
    x1 = [x_ref[rows, :] + _dot(merged[c], wout_ref[...]) for c, rows in enumerate(chunks)]
    h1 = [_rms(x, gffn_ref[...]).astype(BF16) for x in x1]
    x2 = list(x1)
    for ff in FF_PARTS:
        act = [(jax.nn.silu(_dot(h, wgate_ref[:, ff])) * _dot(h, wup_ref[:, ff])).astype(BF16)
               for h in h1]
        x2 = [x + _dot(a, wdown_ref[ff, :]) for x, a in zip(x2, act)]
    h2 = [_rms(x, gple_ref[...]).astype(BF16) for x in x2]
    for c, rows in enumerate(chunks):
        gate = jax.nn.sigmoid(_dot(h2[c], wpg_ref[...]))
        x3 = x2[c] + gate * _dot(p_ref[rows, :].astype(BF16), wpp_ref[...])
        o_ref[rows, :] = _rms(x3, gfin_ref[...])


def _post(u, y_b, g, x2d, p2d, seq, w_pool, pool_scale, w_a, w_b, w_out,
          gain_ffn, w_gate, w_up, w_down, gain_ple, w_pg, w_pp, gain_final):
    t = x2d.shape[0]
    halo_per_tile = POST_TILE // POOL_HALO
    weights = (w_pool, pool_scale, w_a, w_b, w_out, gain_ffn, w_gate, w_up, w_down,
               gain_ple, w_pg, w_pp, gain_final)
    buf_rows = SUBLANES + POOL_HALO + POST_TILE
    return pl.pallas_call(
        functools.partial(_post_kernel, seq),
        grid=(t // POST_TILE,),
        in_specs=[
            _rows(POST_TILE, D_POOL),
            pl.BlockSpec((POOL_HALO, D_POOL), lambda i: (jnp.maximum(i * halo_per_tile - 1, 0), 0)),
            _rows(POST_TILE, D_SB),
            _rows(POST_TILE, 2 * D_MODEL),
            _rows(POST_TILE, D_MODEL),
            _rows(POST_TILE, PLE_DIM),
        ] + [_resident(w.shape) for w in weights],
        out_specs=_rows(POST_TILE, D_MODEL),
        out_shape=jax.ShapeDtypeStruct((t, D_MODEL), F32),
        scratch_shapes=[
            pltpu.VMEM((buf_rows, D_POOL), F32),
            pltpu.VMEM((2, buf_rows, POOL_GROUP_DIM), F32),
        ],
        compiler_params=pltpu.CompilerParams(
            dimension_semantics=("arbitrary",), vmem_limit_bytes=POST_VMEM_LIMIT),
        name="post",
    )(u, u, y_b, g, x2d, p2d, *weights)


def kernel(x, p, norm_mix, w_in, w_pool, pool_scale, w_branch_a, w_branch_b, w_out, norm_ffn,
           w_ffn_gate, w_ffn_up, w_ffn_down, norm_ple, w_ple_gate, w_ple_proj, norm_final):
    batch, seq, d_model = x.shape
    assert w_in.shape[0] == 1 and d_model == D_MODEL
    assert seq % POST_TILE == 0 and seq % Q_BLOCK == 0
    t = batch * seq
    x2d = x.reshape(t, D_MODEL)
    row = lambda v: v.reshape(1, -1).astype(F32)
    bf = lambda w: w[0].astype(BF16)
    u, g, y_b = _mix(x2d, row(norm_mix[0]), bf(w_in), seq)
    out = _post(u, y_b, g, x2d, p[0].reshape(t, PLE_DIM), seq,
                bf(w_pool), row(pool_scale[0]), bf(w_branch_a), bf(w_branch_b), bf(w_out),
                row(norm_ffn[0]), bf(w_ffn_gate), bf(w_ffn_up), bf(w_ffn_down),
                row(norm_ple[0]), bf(w_ple_gate), bf(w_ple_proj), row(norm_final))
    return out.reshape(batch, seq, D_MODEL)
```

```python
import functools

import jax
import jax.numpy as jnp
from jax import lax
from jax.experimental import pallas as pl
from jax.experimental.pallas import tpu as pltpu

D_MODEL = 1024
PLE_DIM = 256
D_POOL = D_MODEL // 2
POOL_WINDOWS = (2, 4, 8, 16)
POOL_GROUP_DIM = D_POOL // len(POOL_WINDOWS)
POOL_HALO = 16
SB_HEADS = 8
SB_HEAD_DIM = 64
D_SB = SB_HEADS * SB_HEAD_DIM
D_FF = ((8 * D_MODEL // 3 + 255) // 256) * 256
D_IN = D_POOL + 3 * D_SB + 2 * D_MODEL
RMS_EPS = 1e-6

LANES = 128
SUBLANES = 8
N_PAIRS = D_SB // LANES
POST_TILE = 512
POST_CHUNK = 256
Q_BLOCK = 256
K_BLOCK = Q_BLOCK
PIECE = 256
FF_PARTS = (slice(0, 6 * PIECE), slice(6 * PIECE, D_FF))
POST_VMEM_LIMIT = 54 * 1024 * 1024
MIX_VMEM_LIMIT = 44 * 1024 * 1024
EXIT_LOG = -104.0

LOG2E = 1.4426950408889634

F32 = jnp.float32
BF16 = jnp.bfloat16


def _dot(a, b):
    return jnp.dot(a, b, preferred_element_type=F32)


def _rms(xf, gain_row):
    inv = lax.rsqrt(jnp.mean(xf * xf, axis=-1, keepdims=True) + RMS_EPS)
    return xf * inv * gain_row


def _resident(shape):
    zeros = (0,) * len(shape)
    return pl.BlockSpec(shape, lambda *_: zeros, pipeline_mode=pl.Buffered(1))


def _rows(tile, width):
    return pl.BlockSpec((tile, width), lambda i: (i, 0))


def _sweep(qs, kblk, vblk, tri, carry, diagonal, filler=None):
    rows = qs.shape[0]
    nz = lax.dot_general(qs, kblk, (((1,), (1,)), ((), ())),
                         preferred_element_type=F32)
    lf = jnp.minimum(nz, 0.0) - jnp.log(1.0 + jnp.exp2(jnp.abs(nz) * (-LOG2E)))
    if diagonal:
        t_loc = lax.broadcasted_iota(jnp.int32, (rows, K_BLOCK), 0) & (Q_BLOCK - 1)
        s_loc = lax.broadcasted_iota(jnp.int32, (rows, K_BLOCK), 1)
        mask = s_loc < t_loc
        lf = jnp.where(mask, lf, 0.0)
    if filler is not None:
        filler()
    suffix = _dot(lf.astype(BF16), tri)
    a = jnp.exp((lf - nz) + suffix)
    if diagonal:
        a = jnp.where(mask, a, 0.0)
    if filler is not None:
        filler()
    pv = _dot(a.astype(BF16), vblk)
    if carry is not None:
        pv = pv * jnp.exp(carry)
    return pv, jnp.sum(lf, axis=1, keepdims=True)


def _mix_kernel(seq, x_ref, gain_ref, w_ref, tri_ref, u_ref, g_ref, yb_ref,
                kc_ref, vc_ref, acc_ref, carry_ref):
    qi = lax.rem(pl.program_id(0), seq // Q_BLOCK)
    row0 = pl.multiple_of(qi * Q_BLOCK, Q_BLOCK)
    h = _rms(x_ref[...], gain_ref[...]).astype(BF16)
    q = (_dot(h, w_ref[:, D_POOL:D_POOL + D_SB]) * (-1.0 / 8.0)).astype(BF16)
    k = _dot(h, w_ref[:, D_POOL + D_SB:D_POOL + 2 * D_SB]).astype(BF16)
    v = _dot(h, w_ref[:, D_POOL + 2 * D_SB:D_POOL + 3 * D_SB]).astype(BF16)
    kc_ref[pl.ds(row0, Q_BLOCK), :] = k
    vc_ref[pl.ds(row0, Q_BLOCK), :] = v

    lane = lax.broadcasted_iota(jnp.int32, (Q_BLOCK, LANES), 1)
    tri = tri_ref[...]
    pairs = [slice(pr * LANES, (pr + 1) * LANES) for pr in range(N_PAIRS)]

    def stacked_q(cols):
        q2 = q[:, cols]
        zero = jnp.zeros_like(q2)
        return jnp.concatenate([jnp.where(lane < SB_HEAD_DIM, q2, zero),
                                jnp.where(lane >= SB_HEAD_DIM, q2, zero)], axis=0)

    def cached(ref, kb, cols):
        return ref[pl.ds(pl.multiple_of(kb * K_BLOCK, K_BLOCK), K_BLOCK), cols]

    has_prev = qi > 0
    prev = jnp.maximum(qi - 1, 0)

    pieces = [(u_ref, c, c) for c in range(0, D_POOL, PIECE)]
    pieces += [(g_ref, c, D_POOL + 3 * D_SB + c) for c in range(0, 2 * D_MODEL, PIECE)]

    def project_pieces(n):
        for _ in range(min(n, len(pieces))):
            ref, c, wc = pieces.pop(0)
            ref[:, c:c + PIECE] = _dot(h, w_ref[:, wc:wc + PIECE])

    for pr, cols in enumerate(pairs):
        qs = stacked_q(cols)
        fill = functools.partial(project_pieces, 1)
        pv_d, tot_d = _sweep(qs, k[:, cols], v[:, cols], tri, None, True, fill)
        pv_p, tot_p = _sweep(qs, cached(kc_ref, prev, cols), cached(vc_ref, prev, cols), tri,
                             tot_d, False, fill)
        acc_ref[pr] = pv_d + jnp.where(has_prev, pv_p, 0.0)
        carry_ref[pr] = tot_d + jnp.where(has_prev, tot_p, 0.0)
    project_pieces(len(pieces))

    def live():
        return (jnp.max(carry_ref[...]) >= EXIT_LOG).astype(jnp.int32)

    def keep_going(state):
        kb, flag = state
        return jnp.logical_and(kb >= 0, flag > 0)

    def body(state):
        kb, _ = state
        for pr, cols in enumerate(pairs):
            carry = carry_ref[pr]
            pv, tot = _sweep(stacked_q(cols), cached(kc_ref, kb, cols), cached(vc_ref, kb, cols),
                             tri, carry, False)
            acc_ref[pr] += pv
            carry_ref[pr] = carry + tot
        return kb - 1, live()

    lax.while_loop(keep_going, body, (qi - 2, live()))

    for pr, cols in enumerate(pairs):
        acc = acc_ref[pr]
        yb_ref[:, cols] = jnp.where(lane < SB_HEAD_DIM, acc[:Q_BLOCK], acc[Q_BLOCK:]).astype(BF16)


def _mix(x2d, gain, w_in, seq):
    t = x2d.shape[0]
    jj = lax.broadcasted_iota(jnp.int32, (K_BLOCK, K_BLOCK), 0)
    ss = lax.broadcasted_iota(jnp.int32, (K_BLOCK, K_BLOCK), 1)
    tri = (jj > ss).astype(BF16)
    return pl.pallas_call(
        functools.partial(_mix_kernel, seq),
        grid=(t // Q_BLOCK,),
        in_specs=[
            _rows(Q_BLOCK, D_MODEL),
            _resident((1, D_MODEL)),
            _resident((D_MODEL, D_IN)),
            _resident((K_BLOCK, K_BLOCK)),
        ],
        out_specs=[
            _rows(Q_BLOCK, D_POOL),
            _rows(Q_BLOCK, 2 * D_MODEL),
            _rows(Q_BLOCK, D_SB),
        ],
        out_shape=[
            jax.ShapeDtypeStruct((t, D_POOL), F32),
            jax.ShapeDtypeStruct((t, 2 * D_MODEL), F32),
            jax.ShapeDtypeStruct((t, D_SB), BF16),
        ],
        scratch_shapes=[
            pltpu.VMEM((seq, D_SB), BF16),
            pltpu.VMEM((seq, D_SB), BF16),
            pltpu.VMEM((N_PAIRS, 2 * Q_BLOCK, LANES), F32),
            pltpu.VMEM((N_PAIRS, 2 * Q_BLOCK, 1), F32),
        ],
        compiler_params=pltpu.CompilerParams(
            dimension_semantics=("arbitrary",), vmem_limit_bytes=MIX_VMEM_LIMIT,
            allow_input_fusion=[False, False, True, False]),
        name="mix",
    )(x2d, gain, w_in, tri)


def _window_sum(ubuf_ref, lvl_ref, cols, w):
    n = POOL_HALO + POST_TILE
    cur = ubuf_ref[SUBLANES:SUBLANES + n, cols] + ubuf_ref[SUBLANES - 1:SUBLANES - 1 + n, cols]
    k, slot = 2, 0
    while k < w:
        lvl_ref[slot, SUBLANES:SUBLANES + n, :] = cur
        cur = cur + lvl_ref[slot, SUBLANES - k:SUBLANES - k + n, :]
        k, slot = 2 * k, 1 - slot
    return cur[POOL_HALO:]


def _post_kernel(seq, u_ref, halo_ref, yb_ref, g_ref, x_ref, p_ref,
                 wpool_ref, pscale_ref, wa_ref, wb_ref, wout_ref,
                 gffn_ref, wgate_ref, wup_ref, wdown_ref,
                 gple_ref, wpg_ref, wpp_ref, gfin_ref, o_ref, ubuf_ref, lvl_ref):
    i = pl.program_id(0)
    t0 = (i * POST_TILE) % seq
    u = u_ref[...]
    halo = halo_ref[...]
    ubuf_ref[0:SUBLANES, :] = jnp.zeros((SUBLANES, D_POOL), F32)
    lvl_ref[:, 0:SUBLANES, :] = jnp.zeros((2, SUBLANES, POOL_GROUP_DIM), F32)
    ubuf_ref[SUBLANES:SUBLANES + POOL_HALO, :] = jnp.where(t0 == 0, jnp.zeros_like(halo), halo)
    ubuf_ref[SUBLANES + POOL_HALO:, :] = u
    pos = t0 + lax.broadcasted_iota(jnp.int32, (POST_TILE, 1), 0)
    pooled = []
    for g, w in enumerate(POOL_WINDOWS):
        cols = slice(g * POOL_GROUP_DIM, (g + 1) * POOL_GROUP_DIM)
        count = jnp.minimum(pos + 1, w).astype(F32)
        pooled.append((_window_sum(ubuf_ref, lvl_ref, cols, w) / count - u[:, cols]).astype(BF16))

    chunks = [slice(r, r + POST_CHUNK) for r in range(0, POST_TILE, POST_CHUNK)]
    groups = range(len(POOL_WINDOWS))

    y_a = [(jnp.concatenate([_dot(pooled[g][rows], wpool_ref[g]) for g in groups], axis=1)
            * pscale_ref[...]).astype(BF16) for rows in chunks]
    merged = [(jax.nn.sigmoid(g_ref[rows, 0:D_MODEL]) * _dot(y_a[c], wa_ref[...])
               + jax.nn.sigmoid(g_ref[rows, D_MODEL:2 * D_MODEL]) * _dot(yb_ref[rows, :], wb_ref[...])
               ).astype(BF16) for c, rows in enumerate(chunks)]
    x1 = [x_ref[rows, :] + _dot(merged[c], wout_ref[...]) for c, rows in enumerate(chunks)]
    h1 = [_rms(x, gffn_ref[...]).astype(BF16) for x in x1]
    x2 = list(x1)
    for ff in FF_PARTS:
        act = [(jax.nn.silu(_dot(h, wgate_ref[:, ff])) * _dot(h, wup_ref[:, ff])).astype(BF16)
               for h in h1]
        x2 = [x + _dot(a, wdown_ref[ff, :]) for x, a in zip(x2, act)]
    h2 = [_rms(x, gple_ref[...]).astype(BF16) for x in x2]
    for c, rows in enumerate(chunks):
        gate = jax.nn.sigmoid(_dot(h2[c], wpg_ref[...]))
        x3 = x2[c] + gate * _dot(p_ref[rows, :].astype(BF16), wpp_ref[...])
        o_ref[rows, :] = _rms(x3, gfin_ref[...])


def _post(u, y_b, g, x2d, p2d, seq, w_pool, pool_scale, w_a, w_b, w_out,
          gain_ffn, w_gate, w_up, w_down, gain_ple, w_pg, w_pp, gain_final):
    t = x2d.shape[0]
    halo_per_tile = POST_TILE // POOL_HALO
    weights = (w_pool, pool_scale, w_a, w_b, w_out, gain_ffn, w_gate, w_up, w_down,
               gain_ple, w_pg, w_pp, gain_final)
    buf_rows = SUBLANES + POOL_HALO + POST_TILE
    return pl.pallas_call(
        functools.partial(_post_kernel, seq),
        grid=(t // POST_TILE,),
        in_specs=[
            _rows(POST_TILE, D_POOL),
            pl.BlockSpec((POOL_HALO, D_POOL), lambda i: (jnp.maximum(i * halo_per_tile - 1, 0), 0)),
            _rows(POST_TILE, D_SB),
            _rows(POST_TILE, 2 * D_MODEL),
            _rows(POST_TILE, D_MODEL),
            _rows(POST_TILE, PLE_DIM),
        ] + [_resident(w.shape) for w in weights],
        out_specs=_rows(POST_TILE, D_MODEL),
        out_shape=jax.ShapeDtypeStruct((t, D_MODEL), F32),
        scratch_shapes=[
            pltpu.VMEM((buf_rows, D_POOL), F32),
            pltpu.VMEM((2, buf_rows, POOL_GROUP_DIM), F32),
        ],
        compiler_params=pltpu.CompilerParams(
            dimension_semantics=("arbitrary",), vmem_limit_bytes=POST_VMEM_LIMIT,
            allow_input_fusion=[False] * 6 + [w.dtype == BF16 for w in weights]),
        name="post",
    )(u, u, y_b, g, x2d, p2d, *weights)


def kernel(x, p, norm_mix, w_in, w_pool, pool_scale, w_branch_a, w_branch_b, w_out, norm_ffn,
           w_ffn_gate, w_ffn_up, w_ffn_down, norm_ple, w_ple_gate, w_ple_proj, norm_final):
    batch, seq, d_model = x.shape
    assert w_in.shape[0] == 1 and d_model == D_MODEL
    assert seq % POST_TILE == 0 and seq % Q_BLOCK == 0
    t = batch * seq
    x2d = x.reshape(t, D_MODEL)
    row = lambda v: v.reshape(1, -1).astype(F32)
    bf = lambda w: w[0].astype(BF16)
    u, g, y_b = _mix(x2d, row(norm_mix[0]), bf(w_in), seq)
    out = _post(u, y_b, g, x2d, p[0].reshape(t, PLE_DIM), seq,
                bf(w_pool), row(pool_scale[0]), bf(w_branch_a), bf(w_branch_b), bf(w_out),
                row(norm_ffn[0]), bf(w_ffn_gate), bf(w_ffn_up), bf(w_ffn_down),
                row(norm_ple[0]), bf(w_ple_gate), bf(w_ple_proj), row(norm_final))
    return out.reshape(batch, seq, D_MODEL)
```
